```python
import jax, jax.numpy as jnp
from jax import lax
import numpy as np

D_MODEL = 2048
BATCH = 32
SEQ = 256
DEPTH = 2
DEC_BATCH = 4
DEC_SEQ = 1024
PAST_LEN = 512

GRID_W = 64
EXPAND = 2
E_MIX = EXPAND * D_MODEL
E_A = E_MIX // 2
E_B = E_MIX - E_A
HEAD_DIM = 128
N_HEADS_A = E_A // HEAD_DIM
NA_ROWS = 8
NA_COLS = 16
NA_QB = 16
NA_KS = NA_QB + NA_COLS
SHORT_CONV = 3
E_C = E_MIX
CONF_CONV = 31
N_EVEN = (DEPTH + 1) // 2
N_ODD = DEPTH // 2
ALPHA = (2.0 * DEPTH) ** 0.25
BETA = (8.0 * DEPTH) ** -0.25
LN_EPS = 1e-5
NEG_BIG = -1e30

kernel_name = "hybrid_natten_shortconv_conformer_dit_step"


def layernorm(x, g, b):
    xf = x.astype(jnp.float32)
    mu = jnp.mean(xf, axis=-1, keepdims=True)
    var = jnp.mean(jnp.square(xf - mu), axis=-1, keepdims=True)
    y = (xf - mu) * lax.rsqrt(var + LN_EPS)
    return (y * g.astype(jnp.float32) + b.astype(jnp.float32)).astype(x.dtype)


def dwconv(x, w):
    k = w.shape[0]
    return lax.conv_general_dilated(
        x, w[:, None, :].astype(x.dtype), window_strides=(1,),
        padding=[(k // 2, k // 2)], dimension_numbers=("NWC", "WIO", "NWC"),
        feature_group_count=x.shape[-1])


def adaln(cond, w_ada, b_ada):
    m = jax.nn.silu(cond) @ w_ada + b_ada
    shift, scale, gate = jnp.split(m, 3, axis=-1)
    return shift, scale, gate


def even_projections(h, w_in):
    return jnp.split(h @ w_in, 8, axis=-1)


def heads(t):
    b, s, _ = t.shape
    return t.reshape(b, s, N_HEADS_A, HEAD_DIM)


def short_conv_branch(bg, cg, xin, gb, conv_w):
    return jax.nn.silu(gb) * bg * dwconv(cg * xin, conv_w)


def context_attention(q, k, v):
    s = jnp.einsum("bqhd,bkhd->bhqk", q, k).astype(jnp.float32) * (HEAD_DIM ** -0.5)
    p = jax.nn.softmax(s, axis=-1).astype(v.dtype)
    o = jnp.einsum("bhqk,bkhd->bqhd", p, v)
    return o.reshape(q.shape[0], q.shape[1], E_A)


def neighbourhood_attention(q, k, v, k_ctx, v_ctx, rpb):
    bsz, t = q.shape[0], q.shape[1]
    rows = t // GRID_W
    kh = min(NA_ROWS, rows)
    ncb = GRID_W // NA_QB
    qg = q.reshape(bsz, rows, ncb, NA_QB, N_HEADS_A, HEAD_DIM)
    kg = k.reshape(bsz, rows, GRID_W, N_HEADS_A, HEAD_DIM)
    vg = v.reshape(bsz, rows, GRID_W, N_HEADS_A, HEAD_DIM)
    r = jnp.arange(rows)
    start_r = jnp.clip(r - kh // 2, 0, rows - kh)
    row_idx = start_r[:, None] + jnp.arange(kh)[None, :]
    c0 = jnp.arange(ncb) * NA_QB
    s_col = jnp.clip(c0 - NA_COLS // 2, 0, GRID_W - NA_KS)
    col_idx = s_col[:, None] + jnp.arange(NA_KS)[None, :]
    kw = kg[:, row_idx[:, :, None, None], col_idx[None, None, :, :]]
    vw = vg[:, row_idx[:, :, None, None], col_idx[None, None, :, :]]
    q_col = c0[:, None] + jnp.arange(NA_QB)[None, :]
    start_c = jnp.clip(q_col - NA_COLS // 2, 0, GRID_W - NA_COLS)
    kc = col_idx[:, None, :]
    valid = (kc >= start_c[:, :, None]) & (kc < start_c[:, :, None] + NA_COLS)
    dr_i = (row_idx - r[:, None]) + (NA_ROWS - 1)
    dc_i = jnp.clip(kc - q_col[:, :, None] + (NA_COLS - 1), 0, 2 * NA_COLS - 2)
    bias = rpb[:, dr_i[:, None, None, :, None], dc_i[None, :, :, None, :]]
    scale = HEAD_DIM ** -0.5
    s_win = jnp.einsum("brnqhd,brinjhd->bhrnqij", qg, kw).astype(jnp.float32) * scale
    s_win = s_win + bias.astype(jnp.float32)[None]
    s_win = jnp.where(valid[:, :, None, :], s_win, NEG_BIG)
    nwin = kh * NA_KS
    s_win = s_win.reshape(s_win.shape[:5] + (nwin,))
    s_ctx = jnp.einsum("brnqhd,blhd->bhrnql", qg, k_ctx).astype(jnp.float32) * scale
    p = jax.nn.softmax(jnp.concatenate([s_win, s_ctx], axis=-1), axis=-1).astype(v.dtype)
    p_win = p[..., :nwin].reshape(p.shape[:5] + (kh, NA_KS))
    p_ctx = p[..., nwin:]
    o = (jnp.einsum("bhrnqij,brinjhd->brnqhd", p_win, vw)
         + jnp.einsum("bhrnql,blhd->brnqhd", p_ctx, v_ctx))
    return o.reshape(bsz, t, E_A)


def even_context(h, w_in, conv_w, w_out):
    q, k, v, ga, bg, cg, xin, gb = even_projections(h, w_in)
    q, k, v = heads(q), heads(k), heads(v)
    o_a = context_attention(q, k, v) * jax.nn.silu(ga)
    o_b = short_conv_branch(bg, cg, xin, gb, conv_w)
    out = jnp.concatenate([o_a, o_b], axis=-1) @ w_out
    return out, k, v


def even_latent(h, k_ctx, v_ctx, w_in, conv_w, rpb, w_out):
    q, k, v, ga, bg, cg, xin, gb = even_projections(h, w_in)
    o_a = neighbourhood_attention(heads(q), heads(k), heads(v), k_ctx, v_ctx, rpb) * jax.nn.silu(ga)
    o_b = short_conv_branch(bg, cg, xin, gb, conv_w)
    return jnp.concatenate([o_a, o_b], axis=-1) @ w_out


def odd_mixer(h, w_in, conv_w, conv_b, ln_g, ln_b, w_out):
    a, b, g = jnp.split(h @ w_in, 3, axis=-1)
    u = a * jax.nn.sigmoid(b)
    u = dwconv(u, conv_w) + conv_b
    u = jax.nn.silu(layernorm(u, ln_g, ln_b))
    return (u * jax.nn.silu(g)) @ w_out


def setup_inputs(seed: int = 0) -> dict:
    key = jax.random.key(seed)
    ks = jax.random.split(key, 24)
    f32 = jnp.float32
    nrm = lambda k, s: jax.random.normal(k, s, f32)
    return {
        "x_prompt": nrm(ks[0], (BATCH, SEQ, D_MODEL)),
        "x_sample": nrm(ks[1], (DEC_BATCH, DEC_SEQ, D_MODEL)),
        "cache_k": nrm(ks[2], (DEC_BATCH, N_EVEN, PAST_LEN, N_HEADS_A, HEAD_DIM)),
        "cache_v": nrm(ks[3], (DEC_BATCH, N_EVEN, PAST_LEN, N_HEADS_A, HEAD_DIM)),
        "c": nrm(ks[4], (DEC_BATCH, D_MODEL)),
        "c_ctx": nrm(ks[5], (D_MODEL,)),
        "w_ada": nrm(ks[6], (DEPTH, D_MODEL, 3 * D_MODEL)) * D_MODEL ** -0.5,
        "b_ada": nrm(ks[7], (DEPTH, 3 * D_MODEL)) * 0.01,
        "ln_g": 1.0 + 0.01 * nrm(ks[8], (DEPTH, D_MODEL)),
        "ln_b": 0.01 * nrm(ks[9], (DEPTH, D_MODEL)),
        "w_in_even": nrm(ks[10], (N_EVEN, D_MODEL, 4 * E_A + 4 * E_B)) * D_MODEL ** -0.5,
        "conv_short_w": nrm(ks[11], (N_EVEN, SHORT_CONV, E_B)) * SHORT_CONV ** -0.5,
        "rpb": 0.02 * nrm(ks[12], (N_EVEN, N_HEADS_A, 2 * NA_ROWS - 1, 2 * NA_COLS - 1)),
        "w_out_even": nrm(ks[13], (N_EVEN, E_MIX, D_MODEL)) * (E_MIX ** -0.5) * BETA,
        "w_in_odd": nrm(ks[14], (N_ODD, D_MODEL, 3 * E_C)) * D_MODEL ** -0.5,
        "conv_conf_w": nrm(ks[15], (N_ODD, CONF_CONV, E_C)) * CONF_CONV ** -0.5,
        "conv_conf_b": 0.01 * nrm(ks[16], (N_ODD, E_C)),
        "ln_conf_g": 1.0 + 0.01 * nrm(ks[17], (N_ODD, E_C)),
        "ln_conf_b": 0.01 * nrm(ks[18], (N_ODD, E_C)),
        "w_out_odd": nrm(ks[19], (N_ODD, E_C, D_MODEL)) * (E_C ** -0.5) * BETA,
    }


def reference(x_prompt, x_sample, cache_k, cache_v, c, c_ctx, w_ada, b_ada, ln_g, ln_b,
              w_in_even, conv_short_w, rpb, w_out_even, w_in_odd, conv_conf_w, conv_conf_b,
              ln_conf_g, ln_conf_b, w_out_odd):
    y_p = x_prompt
    y_s = x_sample
    ks_new, vs_new = [], []
    for l in range(DEPTH):
        sh_c, sc_c, gt_c = adaln(c_ctx, w_ada[l], b_ada[l])
        sh_s, sc_s, gt_s = adaln(c, w_ada[l], b_ada[l])
        sh_s, sc_s, gt_s = sh_s[:, None, :], sc_s[:, None, :], gt_s[:, None, :]
        h_p = y_p * (1.0 + sc_c) + sh_c
        h_s = y_s * (1.0 + sc_s) + sh_s
        if l % 2 == 0:
            e = l // 2
            o_p, k_p, v_p = even_context(h_p, w_in_even[e], conv_short_w[e], w_out_even[e])
            o_s = even_latent(h_s, cache_k[:, e], cache_v[:, e], w_in_even[e], conv_short_w[e],
                              rpb[e], w_out_even[e])
            ks_new.append(k_p)
            vs_new.append(v_p)
        else:
            o = l // 2
            o_p = odd_mixer(h_p, w_in_odd[o], conv_conf_w[o], conv_conf_b[o], ln_conf_g[o],
                            ln_conf_b[o], w_out_odd[o])
            o_s = odd_mixer(h_s, w_in_odd[o], conv_conf_w[o], conv_conf_b[o], ln_conf_g[o],
                            ln_conf_b[o], w_out_odd[o])
        y_p = layernorm(ALPHA * y_p + gt_c * o_p, ln_g[l], ln_b[l])
        y_s = layernorm(ALPHA * y_s + gt_s * o_s, ln_g[l], ln_b[l])
    new_k = jnp.stack(ks_new, axis=1)
    new_v = jnp.stack(vs_new, axis=1)
    return (y_p, y_s, new_k, new_v)
```

```python
import functools

import jax
import jax.numpy as jnp
from jax import lax
from jax.experimental import pallas as pl
from jax.experimental.pallas import tpu as pltpu

F32 = jnp.float32
BF16 = jnp.bfloat16

D_MODEL = 2048
DEPTH = 2
GRID_W = 64
E_MIX = 2 * D_MODEL
E_A = E_MIX // 2
E_B = E_MIX - E_A
HEAD_DIM = 128
N_HEADS_A = E_A // HEAD_DIM
NA_ROWS = 8
NA_COLS = 16
SHORT_CONV = 3
E_C = E_MIX
CONF_CONV = 31
CONF_HALF = CONF_CONV // 2
ALPHA = (2.0 * DEPTH) ** 0.25
LN_EPS = 1e-5
NEG_BIG = -1e30
ATTN_SCALE = HEAD_DIM ** -0.5

N_COND = 8
SUBLANES = 8
HALO = 16
VMEM_LIMIT = 56 * 1024 * 1024

Q_CHUNK_ROWS = 4
KEY_WIN_ROWS = 12


def _params(semantics):
    return pltpu.CompilerParams(dimension_semantics=semantics, vmem_limit_bytes=VMEM_LIMIT)


def _sigmoid(x):
    return 1.0 / (1.0 + jnp.exp(-x))


def _silu(x):
    return x * _sigmoid(x)


def _layernorm(r, g, b):
    mu = jnp.mean(r, axis=-1, keepdims=True)
    d = r - mu
    var = jnp.mean(d * d, axis=-1, keepdims=True)
    return d * lax.rsqrt(var + LN_EPS) * g + b


def _ada_kernel(cond_ref, w_ref, b_ref, o_ref):
    s = _silu(cond_ref[...]).astype(BF16)
    o_ref[0] = jnp.dot(s, w_ref[0].astype(BF16), preferred_element_type=F32) + b_ref[0]


def _ada(cond, w_ada, b_ada):
    depth, d, n3 = w_ada.shape
    tn = 1024
    return pl.pallas_call(
        _ada_kernel,
        grid=(depth, n3 // tn),
        in_specs=[
            pl.BlockSpec((N_COND, d), lambda l, j: (0, 0)),
            pl.BlockSpec((1, d, tn), lambda l, j: (l, 0, j)),
            pl.BlockSpec((1, 1, tn), lambda l, j: (l, 0, j)),
        ],
        out_specs=pl.BlockSpec((1, N_COND, tn), lambda l, j: (l, 0, j)),
        out_shape=jax.ShapeDtypeStruct((depth, N_COND, n3), F32),
        compiler_params=_params(("arbitrary", "arbitrary")),
        name="ada",
    )(cond, w_ada, b_ada.reshape(depth, 1, n3))


def _inproj_even_kernel(x_ref, sh_ref, sc_ref, w_ref, o_ref, hb_ref):
    @pl.when(pl.program_id(1) == 0)
    def _():
        hb_ref[...] = (x_ref[...] * (1.0 + sc_ref[0]) + sh_ref[0]).astype(BF16)

    o_ref[0] = jnp.dot(hb_ref[...], w_ref[...], preferred_element_type=F32)


def _inproj_even(x, mods, w_bf, cond_of_tile, tm, tn):
    rows, d = x.shape
    n = w_bf.shape[1]
    npg = E_A // tn
    return pl.pallas_call(
        _inproj_even_kernel,
        grid=(rows // tm, n // tn),
        in_specs=[
            pl.BlockSpec((tm, d), lambda i, j: (i, 0)),
            pl.BlockSpec((1, 1, d), lambda i, j: (cond_of_tile(i), 0, 0)),
            pl.BlockSpec((1, 1, d), lambda i, j: (cond_of_tile(i), 0, 1)),
            pl.BlockSpec((d, tn), lambda i, j: (0, j)),
        ],
        out_specs=pl.BlockSpec((1, tm, tn), lambda i, j: (j // npg, i, j % npg)),
        out_shape=jax.ShapeDtypeStruct((n // E_A, rows, E_A), F32),
        scratch_shapes=[pltpu.VMEM((tm, d), BF16)],
        compiler_params=_params(("arbitrary", "arbitrary")),
        name="inproj_even",
    )(x, mods, mods, w_bf)


def _attn_ctx_kernel(q_ref, k_ref, v_ref, ga_ref, o_ref):
    for h in range(N_HEADS_A):
        sl = slice(h * HEAD_DIM, (h + 1) * HEAD_DIM)
        q = q_ref[0, :, sl].astype(BF16)
        k = k_ref[0, :, sl].astype(BF16)
        v = v_ref[0, :, sl].astype(BF16)
        s = lax.dot_general(q, k, (((1,), (1,)), ((), ())), preferred_element_type=F32) * ATTN_SCALE
        m = jnp.max(s, axis=-1, keepdims=True)
        p = jnp.exp(s - m)
        l = jnp.sum(p, axis=-1, keepdims=True)
        o = jnp.dot(p.astype(BF16), v, preferred_element_type=F32) / l
        o_ref[:, sl] = (o * _silu(ga_ref[0, :, sl])).astype(BF16)


def _attn_ctx(proj, seq):
    _, rows, e = proj.shape
    spec = lambda g: pl.BlockSpec((1, seq, e), lambda b, g=g: (g, b, 0))
    return pl.pallas_call(
        _attn_ctx_kernel,
        grid=(rows // seq,),
        in_specs=[spec(0), spec(1), spec(2), spec(3)],
        out_specs=pl.BlockSpec((seq, e), lambda b: (b, 0)),
        out_shape=jax.ShapeDtypeStruct((rows, e), BF16),
        compiler_params=_params(("arbitrary",)),
        name="attn_ctx",
    )(proj, proj, proj, proj)


def _key_window_start(chunk, n_rows):
    lo = max(chunk * Q_CHUNK_ROWS - NA_ROWS // 2, 0)
    return min(lo - lo % Q_CHUNK_ROWS, n_rows - KEY_WIN_ROWS)


def _build_bias(rpb_ref, bias_ref, head, n_rows):
    pair_w = 2 * GRID_W
    qc = lax.broadcasted_iota(jnp.int32, (GRID_W, pair_w), 0)
    lane = lax.broadcasted_iota(jnp.int32, (GRID_W, pair_w), 1)
    kc = lane % GRID_W
    rel = kc - qc + (NA_COLS - 1)
    start_c = jnp.clip(qc - NA_COLS // 2, 0, GRID_W - NA_COLS)
    valid = (kc >= start_c) & (kc < start_c + NA_COLS)
    n_dc = 2 * NA_COLS - 1
    neg = jnp.full((GRID_W, pair_w), NEG_BIG, F32)
    toeplitz = []
    for i in range(2 * NA_ROWS - 1):
        t = neg
        for dcol in range(n_dc):
            t = jnp.where(rel == dcol, rpb_ref[head, i * n_dc + dcol], t)
        toeplitz.append(jnp.where(valid, t, NEG_BIG))
    for r in range(n_rows):
        w0 = _key_window_start(r // Q_CHUNK_ROWS, n_rows)
        start_r = min(max(r - NA_ROWS // 2, 0), n_rows - NA_ROWS)

        def block(kr):
            inside = start_r <= kr < start_r + NA_ROWS
            return toeplitz[kr - r + NA_ROWS - 1] if inside else neg

        for pair in range(KEY_WIN_ROWS // 2):
            blk = jnp.where(lane < GRID_W, block(w0 + 2 * pair), block(w0 + 2 * pair + 1))
            bias_ref[r * GRID_W:(r + 1) * GRID_W, pair * pair_w:(pair + 1) * pair_w] = blk


def _attn_nbr_kernel(rpb_ref, q_ref, k_ref, v_ref, ga_ref, kc_ref, vc_ref, o_ref, bias_ref, *, n_rows):
    head = pl.program_id(0)

    @pl.when(pl.program_id(1) == 0)
    def _():
        _build_bias(rpb_ref, bias_ref, head, n_rows)

    k_ctx = kc_ref[0].astype(BF16)
    v_ctx = vc_ref[0].astype(BF16)
    qn = Q_CHUNK_ROWS * GRID_W
    kn = KEY_WIN_ROWS * GRID_W
    contract_last = (((1,), (1,)), ((), ()))
    for chunk in range(n_rows // Q_CHUNK_ROWS):
        qs = slice(chunk * qn, (chunk + 1) * qn)
        k0 = _key_window_start(chunk, n_rows) * GRID_W
        q = q_ref[0, qs, :].astype(BF16)
        k_win = k_ref[0, k0:k0 + kn, :].astype(BF16)
        v_win = v_ref[0, k0:k0 + kn, :].astype(BF16)
        s_win = lax.dot_general(q, k_win, contract_last, preferred_element_type=F32) * ATTN_SCALE
        s_win = s_win + bias_ref[qs, :]
        s_ctx = lax.dot_general(q, k_ctx, contract_last, preferred_element_type=F32) * ATTN_SCALE
        m = jnp.maximum(jnp.max(s_win, axis=-1, keepdims=True), jnp.max(s_ctx, axis=-1, keepdims=True))
        p_win = jnp.exp(s_win - m)
        p_ctx = jnp.exp(s_ctx - m)
        l = jnp.sum(p_win, axis=-1, keepdims=True) + jnp.sum(p_ctx, axis=-1, keepdims=True)
        o = (jnp.dot(p_win.astype(BF16), v_win, preferred_element_type=F32)
             + jnp.dot(p_ctx.astype(BF16), v_ctx, preferred_element_type=F32)) / l
        o_ref[qs, :] = (o * _silu(ga_ref[0, qs, :])).astype(BF16)


def _attn_nbr(proj, k_ctx, v_ctx, rpb, seq):
    _, rows, e = proj.shape
    past = k_ctx.shape[1]
    n_rows = seq // GRID_W
    assert n_rows % Q_CHUNK_ROWS == 0 and n_rows >= KEY_WIN_ROWS and NA_ROWS <= n_rows
    n_bias = (2 * NA_ROWS - 1) * (2 * NA_COLS - 1)
    spec = lambda g: pl.BlockSpec((1, seq, HEAD_DIM), lambda h, b, g=g: (g, b, h))
    ctx_spec = pl.BlockSpec((1, past, HEAD_DIM), lambda h, b: (b, 0, h))
    return pl.pallas_call(
        functools.partial(_attn_nbr_kernel, n_rows=n_rows),
        grid=(N_HEADS_A, rows // seq),
        in_specs=[
            pl.BlockSpec(memory_space=pltpu.SMEM),
            spec(0), spec(1), spec(2), spec(3), ctx_spec, ctx_spec,
        ],
        out_specs=pl.BlockSpec((seq, HEAD_DIM), lambda h, b: (b, h)),
        out_shape=jax.ShapeDtypeStruct((rows, e), BF16),
        scratch_shapes=[pltpu.VMEM((seq, KEY_WIN_ROWS * GRID_W), F32)],
        compiler_params=_params(("arbitrary", "arbitrary")),
        name="attn_nbr",
    )(rpb.reshape(N_HEADS_A, n_bias), proj, proj, proj, proj, k_ctx, v_ctx)


def _outproj_even_kernel(oa_ref, bg_ref, cg_ref, xin_ref, gb_ref, cgp_ref, xinp_ref, cgn_ref, xinn_ref,
                         cw_ref, w_ref, x_ref, gate_ref, lng_ref, lnb_ref, o_ref, acc_ref,
                         *, nh, nk, tm, seq):
    i = pl.program_id(0)
    kk = pl.program_id(1)

    @pl.when(kk == 0)
    def _():
        acc_ref[...] = jnp.zeros_like(acc_ref)

    @pl.when(kk < nh)
    def _():
        acc_ref[...] += jnp.dot(oa_ref[...], w_ref[...], preferred_element_type=F32)

    @pl.when(kk >= nh)
    def _():
        z = cg_ref[0] * xin_ref[0]
        z_before = cgp_ref[0, SUBLANES - 1:SUBLANES, :] * xinp_ref[0, SUBLANES - 1:SUBLANES, :]
        z_after = cgn_ref[0, 0:1, :] * xinn_ref[0, 0:1, :]
        row = lax.broadcasted_iota(jnp.int32, z.shape, 0)
        pos = (i * tm + row) % seq
        z_prev = jnp.where(row == 0, z_before, pltpu.roll(z, 1, axis=0))
        z_prev = jnp.where(pos == 0, 0.0, z_prev)
        z_next = jnp.where(row == tm - 1, z_after, pltpu.roll(z, tm - 1, axis=0))
        z_next = jnp.where(pos == seq - 1, 0.0, z_next)
        conv = z_prev * cw_ref[0:1, :] + z * cw_ref[1:2, :] + z_next * cw_ref[2:3, :]
        ob = _silu(gb_ref[0]) * bg_ref[0] * conv
        acc_ref[...] += jnp.dot(ob.astype(BF16), w_ref[...], preferred_element_type=F32)

    @pl.when(kk == nk - 1)
    def _():
        r = ALPHA * x_ref[...] + gate_ref[0] * acc_ref[...]
        o_ref[...] = _layernorm(r, lng_ref[...], lnb_ref[...])


def _outproj_even(oa, proj, conv_w, w_bf, x, mods, ln_g, ln_b, cond_of_tile, seq, tm, tk):
    rows, d = x.shape
    nh = E_A // tk
    nk = E_MIX // tk
    n_sub = rows // SUBLANES
    kb = lambda kk: jnp.maximum(kk - nh, 0)
    grp = lambda g: pl.BlockSpec((1, tm, tk), lambda i, kk, g=g: (g, i, kb(kk)))
    before = lambda g: pl.BlockSpec(
        (1, SUBLANES, tk), lambda i, kk, g=g: (g, jnp.maximum(i * (tm // SUBLANES) - 1, 0), kb(kk)))
    after = lambda g: pl.BlockSpec(
        (1, SUBLANES, tk), lambda i, kk, g=g: (g, jnp.minimum((i + 1) * (tm // SUBLANES), n_sub - 1), kb(kk)))
    row_vec = pl.BlockSpec((1, d), lambda i, kk: (0, 0))
    return pl.pallas_call(
        functools.partial(_outproj_even_kernel, nh=nh, nk=nk, tm=tm, seq=seq),
        grid=(rows // tm, nk),
        in_specs=[
            pl.BlockSpec((tm, tk), lambda i, kk: (i, jnp.minimum(kk, nh - 1))),
            grp(4), grp(5), grp(6), grp(7),
            before(5), before(6), after(5), after(6),
            pl.BlockSpec((SHORT_CONV, tk), lambda i, kk: (0, kb(kk))),
            pl.BlockSpec((tk, d), lambda i, kk: (kk, 0)),
            pl.BlockSpec((tm, d), lambda i, kk: (i, 0)),
            pl.BlockSpec((1, 1, d), lambda i, kk: (cond_of_tile(i), 0, 2)),
            row_vec, row_vec,
        ],
        out_specs=pl.BlockSpec((tm, d), lambda i, kk: (i, 0)),
        out_shape=jax.ShapeDtypeStruct((rows, d), F32),
        scratch_shapes=[pltpu.VMEM((tm, d), F32)],
        compiler_params=_params(("arbitrary", "arbitrary")),
        name="outproj_even",
    )(oa, proj, proj, proj, proj, proj, proj, proj, proj, conv_w, w_bf, x, mods,
      ln_g.reshape(1, d), ln_b.reshape(1, d))


def _inproj_odd_kernel(x_ref, sh_ref, sc_ref, wa_ref, wb_ref, wg_ref, u_ref, sg_ref, hb_ref):
    @pl.when(pl.program_id(1) == 0)
    def _():
        hb_ref[...] = (x_ref[...] * (1.0 + sc_ref[0]) + sh_ref[0]).astype(BF16)

    h = hb_ref[...]
    a = jnp.dot(h, wa_ref[...], preferred_element_type=F32)
    b = jnp.dot(h, wb_ref[...], preferred_element_type=F32)
    u_ref[...] = a * _sigmoid(b)
    sg_ref[...] = _silu(jnp.dot(h, wg_ref[...], preferred_element_type=F32))


def _inproj_odd(x, mods, w_bf, cond_of_tile, tm, tn):
    rows, d = x.shape
    nj = E_C // tn
    wspec = lambda part: pl.BlockSpec((d, tn), lambda i, j, part=part: (0, part * nj + j))
    out = pl.BlockSpec((tm, tn), lambda i, j: (i, j))
    return pl.pallas_call(
        _inproj_odd_kernel,
        grid=(rows // tm, nj),
        in_specs=[
            pl.BlockSpec((tm, d), lambda i, j: (i, 0)),
            pl.BlockSpec((1, 1, d), lambda i, j: (cond_of_tile(i), 0, 0)),
            pl.BlockSpec((1, 1, d), lambda i, j: (cond_of_tile(i), 0, 1)),
            wspec(0), wspec(1), wspec(2),
        ],
        out_specs=[out, out],
        out_shape=[jax.ShapeDtypeStruct((rows, E_C), F32)] * 2,
        scratch_shapes=[pltpu.VMEM((tm, d), BF16)],
        compiler_params=_params(("arbitrary", "arbitrary")),
        name="inproj_odd",
    )(x, mods, mods, w_bf, w_bf, w_bf)


CONV_ROWS = 16
N_SHIFT = SUBLANES


def _outproj_odd_kernel(u_ref, ub_ref, ua_ref, cw_ref, cb_ref, cng_ref, cnb_ref, sg_ref, w_ref, x_ref,
                        gate_ref, lng_ref, lnb_ref, o_ref,
                        pad_ref, shift_ref, wrep_ref, conv_ref, sum_ref, mu_ref, rstd_ref, acc_ref,
                        *, nk, tm, tk, seq):
    i = pl.program_id(0)
    kc = pl.program_id(1)
    span = tm + 2 * HALO - SUBLANES

    @pl.when(kc < nk)
    def _():
        tile_pos = (i * tm) % seq
        pad_ref[0:HALO, :] = jnp.where(tile_pos == 0, 0.0, ub_ref[...])
        pad_ref[HALO:HALO + tm, :] = u_ref[...]
        pad_ref[HALO + tm:2 * HALO + tm, :] = jnp.where(tile_pos + tm == seq, 0.0, ua_ref[...])
        for s in range(N_SHIFT):
            shift_ref[s] = pad_ref[s:s + span, :]
        for k in range(CONF_CONV):
            wrep_ref[k] = jnp.broadcast_to(cw_ref[k:k + 1, :], (CONV_ROWS, tk))

        @pl.when(kc == 0)
        def _():
            sum_ref[...] = jnp.zeros_like(sum_ref)

        def body(rb, carry):
            base = pl.multiple_of(rb * CONV_ROWS, CONV_ROWS)
            acc = jnp.broadcast_to(cb_ref[...], (CONV_ROWS, tk))
            for k in range(CONF_CONV):
                off = k + HALO - CONF_HALF
                sh, al = off % N_SHIFT, off - off % N_SHIFT
                acc = acc + shift_ref[sh, pl.ds(base + al, CONV_ROWS), :] * wrep_ref[k]
            conv_ref[kc, pl.ds(base, CONV_ROWS), :] = acc
            sum_ref[pl.ds(base, CONV_ROWS), :] += jnp.sum(acc, axis=-1, keepdims=True)
            return carry

        lax.fori_loop(0, tm // CONV_ROWS, body, 0)

    @pl.when(kc == nk)
    def _():
        mu = sum_ref[...] * (1.0 / E_C)
        var = jnp.zeros_like(mu)
        for c in range(nk):
            dlt = conv_ref[c] - mu
            var = var + jnp.sum(dlt * dlt, axis=-1, keepdims=True)
        mu_ref[...] = mu
        rstd_ref[...] = lax.rsqrt(var * (1.0 / E_C) + LN_EPS)
        acc_ref[...] = jnp.zeros_like(acc_ref)

    @pl.when(kc >= nk)
    def _():
        v = (conv_ref[kc - nk] - mu_ref[...]) * rstd_ref[...] * cng_ref[...] + cnb_ref[...]
        v = _silu(v) * sg_ref[...]
        acc_ref[...] += jnp.dot(v.astype(BF16), w_ref[...], preferred_element_type=F32)

    @pl.when(kc == 2 * nk - 1)
    def _():
        r = ALPHA * x_ref[...] + gate_ref[0] * acc_ref[...]
        o_ref[...] = _layernorm(r, lng_ref[...], lnb_ref[...])


def _outproj_odd(u, sg, conv_w, conv_b, cn_g, cn_b, w_bf, x, mods, ln_g, ln_b, cond_of_tile, seq, tm, tk):
    rows, d = x.shape
    nk = E_C // tk
    n_halo = rows // HALO
    first = lambda kc: jnp.minimum(kc, nk - 1)
    second = lambda kc: jnp.maximum(kc - nk, 0)
    row_vec = pl.BlockSpec((1, d), lambda i, kc: (0, 0))
    chan_vec = pl.BlockSpec((1, tk), lambda i, kc: (0, second(kc)))
    return pl.pallas_call(
        functools.partial(_outproj_odd_kernel, nk=nk, tm=tm, tk=tk, seq=seq),
        grid=(rows // tm, 2 * nk),
        in_specs=[
            pl.BlockSpec((tm, tk), lambda i, kc: (i, first(kc))),
            pl.BlockSpec((HALO, tk), lambda i, kc: (jnp.maximum(i * (tm // HALO) - 1, 0), first(kc))),
            pl.BlockSpec((HALO, tk), lambda i, kc: (jnp.minimum((i + 1) * (tm // HALO), n_halo - 1), first(kc))),
            pl.BlockSpec((CONF_CONV, tk), lambda i, kc: (0, first(kc))),
            pl.BlockSpec((1, tk), lambda i, kc: (0, first(kc))),
            chan_vec, chan_vec,
            pl.BlockSpec((tm, tk), lambda i, kc: (i, second(kc))),
            pl.BlockSpec((tk, d), lambda i, kc: (second(kc), 0)),
            pl.BlockSpec((tm, d), lambda i, kc: (i, 0)),
            pl.BlockSpec((1, 1, d), lambda i, kc: (cond_of_tile(i), 0, 2)),
            row_vec, row_vec,
        ],
        out_specs=pl.BlockSpec((tm, d), lambda i, kc: (i, 0)),
        out_shape=jax.ShapeDtypeStruct((rows, d), F32),
        scratch_shapes=[
            pltpu.VMEM((tm + 2 * HALO, tk), F32),
            pltpu.VMEM((N_SHIFT, tm + 2 * HALO - SUBLANES, tk), F32),
            pltpu.VMEM((CONF_CONV, CONV_ROWS, tk), F32),
            pltpu.VMEM((nk, tm, tk), F32),
            pltpu.VMEM((tm, 1), F32),
            pltpu.VMEM((tm, 1), F32),
            pltpu.VMEM((tm, 1), F32),
            pltpu.VMEM((tm, d), F32),
        ],
        compiler_params=_params(("arbitrary", "arbitrary")),
        name="outproj_odd",
    )(u, u, u, conv_w, conv_b.reshape(1, E_C), cn_g.reshape(1, E_C), cn_b.reshape(1, E_C), sg, w_bf, x, mods,
      ln_g.reshape(1, d), ln_b.reshape(1, d))


def _tile(rows, want):
    return min(rows, want)


def kernel(x_prompt, x_sample, cache_k, cache_v, c, c_ctx, w_ada, b_ada, ln_g, ln_b, w_in_even, conv_short_w,
           rpb, w_out_even, w_in_odd, conv_conf_w, conv_conf_b, ln_conf_g, ln_conf_b, w_out_odd):
    batch, seq, d = x_prompt.shape
    dec_batch, dec_seq, _ = x_sample.shape
    past = cache_k.shape[2]
    assert 1 + dec_batch <= N_COND and d == D_MODEL

    cond = jnp.zeros((N_COND, d), F32).at[0].set(c_ctx).at[1:1 + dec_batch].set(c)
    mods_all = _ada(cond, w_ada, b_ada)

    y_p = x_prompt.reshape(batch * seq, d)
    y_s = x_sample.reshape(dec_batch * dec_seq, d)
    new_k, new_v = [], []
    for layer in range(DEPTH):
        mods = mods_all[layer].reshape(N_COND, 1, 3 * d)
        cond_p = lambda i: 0
        streams = ((y_p, seq, cond_p), (y_s, dec_seq, None))
        if layer % 2 == 0:
            e = layer // 2
            w_in = w_in_even[e].astype(BF16)
            w_out = w_out_even[e].astype(BF16)
            outs = []
            for y, sq, cond_fn in streams:
                rows = y.shape[0]
                tm = _tile(rows, 1024)
                cf = cond_fn or (lambda i, tm=tm: 1 + (i * tm) // dec_seq)
                proj = _inproj_even(y, mods, w_in, cf, tm, 512)
                if cond_fn is not None:
                    oa = _attn_ctx(proj, sq)
                    new_k.append(proj[1].reshape(batch, seq, N_HEADS_A, HEAD_DIM))
                    new_v.append(proj[2].reshape(batch, seq, N_HEADS_A, HEAD_DIM))
                else:
                    k_ctx = cache_k[:, e].reshape(dec_batch, past, E_A)
                    v_ctx = cache_v[:, e].reshape(dec_batch, past, E_A)
                    oa = _attn_nbr(proj, k_ctx, v_ctx, rpb[e], sq)
                tmo = _tile(rows, 256)
                cfo = cond_fn or (lambda i, tm=tmo: 1 + (i * tm) // dec_seq)
                outs.append(_outproj_even(oa, proj, conv_short_w[e], w_out, y, mods, ln_g[layer], ln_b[layer],
                                          cfo, sq, tmo, 512))
            y_p, y_s = outs
        else:
            o = layer // 2
            w_in = w_in_odd[o].astype(BF16)
            w_out = w_out_odd[o].astype(BF16)
            outs = []
            for y, sq, cond_fn in streams:
                rows = y.shape[0]
                tm = _tile(rows, 1024)
                cf = cond_fn or (lambda i, tm=tm: 1 + (i * tm) // dec_seq)
                u, sg = _inproj_odd(y, mods, w_in, cf, tm, 256)
                tmo = _tile(rows, 256)
                cfo = cond_fn or (lambda i, tm=tmo: 1 + (i * tm) // dec_seq)
                outs.append(_outproj_odd(u, sg, conv_conf_w[o], conv_conf_b[o], ln_conf_g[o], ln_conf_b[o], w_out,
                                         y, mods, ln_g[layer], ln_b[layer], cfo, sq, tmo, 512))
            y_p, y_s = outs
    return (y_p.reshape(batch, seq, d), y_s.reshape(dec_batch, dec_seq, d),
            jnp.stack(new_k, axis=1), jnp.stack(new_v, axis=1))
```

```python
import functools

import jax
import jax.numpy as jnp
from jax import lax
from jax.experimental import pallas as pl
from jax.experimental.pallas import tpu as pltpu

F32 = jnp.float32
BF16 = jnp.bfloat16

D_MODEL = 2048
DEPTH = 2
GRID_W = 64
E_MIX = 2 * D_MODEL
E_A = E_MIX // 2
E_B = E_MIX - E_A
HEAD_DIM = 128
N_HEADS_A = E_A // HEAD_DIM
NA_ROWS = 8
NA_COLS = 16
SHORT_CONV = 3
E_C = E_MIX
CONF_CONV = 31
CONF_HALF = CONF_CONV // 2
ALPHA = (2.0 * DEPTH) ** 0.25
LN_EPS = 1e-5
NEG_BIG = -1e30
ATTN_SCALE = HEAD_DIM ** -0.5

N_COND = 8
SUBLANES = 8
LANES = 128
HALO = 16
VMEM_LIMIT = 56 * 1024 * 1024

Q_CHUNK_ROWS = 4
KEY_WIN_ROWS = 12


def _params(semantics):
    return pltpu.CompilerParams(dimension_semantics=semantics, vmem_limit_bytes=VMEM_LIMIT)


def _resident(block_shape, index_map):
    return pl.BlockSpec(block_shape, index_map, pipeline_mode=pl.Buffered(1))


def _sigmoid(x):
    return 1.0 / (1.0 + jnp.exp(-x))


def _silu(x):
    return x * _sigmoid(x)


def _layernorm(r, g, b):
    mu = jnp.mean(r, axis=-1, keepdims=True)
    d = r - mu
    var = jnp.mean(d * d, axis=-1, keepdims=True)
    return d * lax.rsqrt(var + LN_EPS) * g + b


def _ada_kernel(cond_ref, w_ref, b_ref, o_ref):
    s = _silu(cond_ref[...]).astype(BF16)
    o_ref[0] = jnp.dot(s, w_ref[0].astype(BF16), preferred_element_type=F32) + b_ref[0]


def _ada(cond, w_ada, b_ada):
    depth, d, n3 = w_ada.shape
    tn = 1024
    return pl.pallas_call(
        _ada_kernel,
        grid=(depth, n3 // tn),
        in_specs=[
            pl.BlockSpec((N_COND, d), lambda l, j: (0, 0)),
            pl.BlockSpec((1, d, tn), lambda l, j: (l, 0, j)),
            pl.BlockSpec((1, 1, tn), lambda l, j: (l, 0, j)),
        ],
        out_specs=pl.BlockSpec((1, N_COND, tn), lambda l, j: (l, 0, j)),
        out_shape=jax.ShapeDtypeStruct((depth, N_COND, n3), F32),
        compiler_params=_params(("arbitrary", "arbitrary")),
        name="ada",
    )(cond, w_ada, b_ada.reshape(depth, 1, n3))


GROUP_K, GROUP_V = 1, 2
N_REST = 6
REST_Q, REST_GA, REST_B, REST_C, REST_XIN, REST_GB = range(N_REST)


def _inproj_even_kernel(x_ref, sh_ref, sc_ref, w_ref, rest_ref, k_ref, v_ref, hb_ref, *, npg):
    j = pl.program_id(1)

    @pl.when(j == 0)
    def _():
        hb_ref[...] = (x_ref[...] * (1.0 + sc_ref[0]) + sh_ref[0]).astype(BF16)

    group = j // npg

    @pl.when(group == GROUP_K)
    def _():
        k_ref[...] = jnp.dot(hb_ref[...], w_ref[...], preferred_element_type=F32)

    @pl.when(group == GROUP_V)
    def _():
        v_ref[...] = jnp.dot(hb_ref[...], w_ref[...], preferred_element_type=F32)

    @pl.when((group != GROUP_K) & (group != GROUP_V))
    def _():
        rest_ref[0] = jnp.dot(hb_ref[...], w_ref[...], preferred_element_type=F32)


def _inproj_even(x, mods, w_bf, cond_of_tile, tm, tn):
    rows, d = x.shape
    n = w_bf.shape[1]
    npg = E_A // tn

    def rest_map(i, j):
        group = j // npg
        slot = jnp.where(group == 0, 0, jnp.maximum(group - 2, 0))
        tile = jnp.where((group == GROUP_K) | (group == GROUP_V), npg - 1, j % npg)
        return slot, i, tile

    kv_map = lambda g: (lambda i, j: (i, jnp.clip(j - g * npg, 0, npg - 1)))
    return pl.pallas_call(
        functools.partial(_inproj_even_kernel, npg=npg),
        grid=(rows // tm, n // tn),
        in_specs=[
            pl.BlockSpec((tm, d), lambda i, j: (i, 0)),
            pl.BlockSpec((1, 1, d), lambda i, j: (cond_of_tile(i), 0, 0)),
            pl.BlockSpec((1, 1, d), lambda i, j: (cond_of_tile(i), 0, 1)),
            pl.BlockSpec((d, tn), lambda i, j: (0, j)),
        ],
        out_specs=[
            pl.BlockSpec((1, tm, tn), rest_map),
            pl.BlockSpec((tm, tn), kv_map(GROUP_K)),
            pl.BlockSpec((tm, tn), kv_map(GROUP_V)),
        ],
        out_shape=[
            jax.ShapeDtypeStruct((N_REST, rows, E_A), F32),
            jax.ShapeDtypeStruct((rows, E_A), F32),
            jax.ShapeDtypeStruct((rows, E_A), F32),
        ],
        scratch_shapes=[pltpu.VMEM((tm, d), BF16)],
        compiler_params=_params(("arbitrary", "arbitrary")),
        name="inproj_even",
    )(x, mods, mods, w_bf)


def _attn_ctx_kernel(q_ref, k_ref, v_ref, ga_ref, o_ref):
    for h in range(N_HEADS_A):
        sl = slice(h * HEAD_DIM, (h + 1) * HEAD_DIM)
        q = q_ref[0, :, sl].astype(BF16)
        k = k_ref[:, sl].astype(BF16)
        v = v_ref[:, sl].astype(BF16)
        s = lax.dot_general(q, k, (((1,), (1,)), ((), ())), preferred_element_type=F32) * ATTN_SCALE
        m = jnp.max(s, axis=-1, keepdims=True)
        p = jnp.exp(s - m)
        l = jnp.sum(p, axis=-1, keepdims=True)
        o = jnp.dot(p.astype(BF16), v, preferred_element_type=F32) / l
        o_ref[:, sl] = (o * _silu(ga_ref[0, :, sl])).astype(BF16)


def _attn_ctx(rest, k, v, seq):
    _, rows, e = rest.shape
    grp = lambda g: pl.BlockSpec((1, seq, e), lambda b, g=g: (g, b, 0))
    flat = pl.BlockSpec((seq, e), lambda b: (b, 0))
    return pl.pallas_call(
        _attn_ctx_kernel,
        grid=(rows // seq,),
        in_specs=[grp(REST_Q), flat, flat, grp(REST_GA)],
        out_specs=flat,
        out_shape=jax.ShapeDtypeStruct((rows, e), BF16),
        compiler_params=_params(("arbitrary",)),
        name="attn_ctx",
    )(rest, k, v, rest)


def _key_window_start(chunk, n_rows):
    lo = max(chunk * Q_CHUNK_ROWS - NA_ROWS // 2, 0)
    return min(lo - lo % Q_CHUNK_ROWS, n_rows - KEY_WIN_ROWS)


def _build_bias(rpb_ref, bias_ref, head, n_rows):
    pair_w = 2 * GRID_W
    qc = lax.broadcasted_iota(jnp.int32, (GRID_W, pair_w), 0)
    lane = lax.broadcasted_iota(jnp.int32, (GRID_W, pair_w), 1)
    kc = lane % GRID_W
    rel = kc - qc + (NA_COLS - 1)
    start_c = jnp.clip(qc - NA_COLS // 2, 0, GRID_W - NA_COLS)
    valid = (kc >= start_c) & (kc < start_c + NA_COLS)
    n_dc = 2 * NA_COLS - 1
    neg = jnp.full((GRID_W, pair_w), NEG_BIG, F32)
    toeplitz = []
    for i in range(2 * NA_ROWS - 1):
        t = neg
        for dcol in range(n_dc):
            t = jnp.where(rel == dcol, rpb_ref[head, i * n_dc + dcol], t)
        toeplitz.append(jnp.where(valid, t, NEG_BIG))
    for r in range(n_rows):
        w0 = _key_window_start(r // Q_CHUNK_ROWS, n_rows)
        start_r = min(max(r - NA_ROWS // 2, 0), n_rows - NA_ROWS)

        def block(kr):
            inside = start_r <= kr < start_r + NA_ROWS
            return toeplitz[kr - r + NA_ROWS - 1] if inside else neg

        for pair in range(KEY_WIN_ROWS // 2):
            blk = jnp.where(lane < GRID_W, block(w0 + 2 * pair), block(w0 + 2 * pair + 1))
            bias_ref[r * GRID_W:(r + 1) * GRID_W, pair * pair_w:(pair + 1) * pair_w] = blk


def _attn_nbr_kernel(rpb_ref, q_ref, k_ref, v_ref, ga_ref, kc_ref, vc_ref, o_ref, bias_ref, *, n_rows):
    head = pl.program_id(0)

    @pl.when(pl.program_id(1) == 0)
    def _():
        _build_bias(rpb_ref, bias_ref, head, n_rows)

    k_ctx = kc_ref[0].astype(BF16)
    v_ctx = vc_ref[0].astype(BF16)
    qn = Q_CHUNK_ROWS * GRID_W
    kn = KEY_WIN_ROWS * GRID_W
    contract_last = (((1,), (1,)), ((), ()))
    for chunk in range(n_rows // Q_CHUNK_ROWS):
        qs = slice(chunk * qn, (chunk + 1) * qn)
        k0 = _key_window_start(chunk, n_rows) * GRID_W
        q = q_ref[0, qs, :].astype(BF16)
        k_win = k_ref[k0:k0 + kn, :].astype(BF16)
        v_win = v_ref[k0:k0 + kn, :].astype(BF16)
        s_win = lax.dot_general(q, k_win, contract_last, preferred_element_type=F32) * ATTN_SCALE
        s_win = s_win + bias_ref[qs, :]
        s_ctx = lax.dot_general(q, k_ctx, contract_last, preferred_element_type=F32) * ATTN_SCALE
        m = jnp.maximum(jnp.max(s_win, axis=-1, keepdims=True), jnp.max(s_ctx, axis=-1, keepdims=True))
        p_win = jnp.exp(s_win - m)
        p_ctx = jnp.exp(s_ctx - m)
        l = jnp.sum(p_win, axis=-1, keepdims=True) + jnp.sum(p_ctx, axis=-1, keepdims=True)
        o = (jnp.dot(p_win.astype(BF16), v_win, preferred_element_type=F32)
             + jnp.dot(p_ctx.astype(BF16), v_ctx, preferred_element_type=F32)) / l
        o_ref[qs, :] = (o * _silu(ga_ref[0, qs, :])).astype(BF16)


def _attn_nbr(rest, k, v, k_ctx, v_ctx, rpb, seq, past, even_idx):
    _, rows, e = rest.shape
    n_rows = seq // GRID_W
    assert n_rows % Q_CHUNK_ROWS == 0 and n_rows >= KEY_WIN_ROWS and NA_ROWS <= n_rows
    n_bias = (2 * NA_ROWS - 1) * (2 * NA_COLS - 1)
    grp = lambda g: pl.BlockSpec((1, seq, HEAD_DIM), lambda h, b, g=g: (g, b, h))
    flat = pl.BlockSpec((seq, HEAD_DIM), lambda h, b: (b, h))
    ctx_spec = pl.BlockSpec((1, past, HEAD_DIM), lambda h, b: (b, even_idx, h))
    return pl.pallas_call(
        functools.partial(_attn_nbr_kernel, n_rows=n_rows),
        grid=(N_HEADS_A, rows // seq),
        in_specs=[
            pl.BlockSpec(memory_space=pltpu.SMEM),
            grp(REST_Q), flat, flat, grp(REST_GA), ctx_spec, ctx_spec,
        ],
        out_specs=flat,
        out_shape=jax.ShapeDtypeStruct((rows, e), BF16),
        scratch_shapes=[pltpu.VMEM((seq, KEY_WIN_ROWS * GRID_W), F32)],
        compiler_params=_params(("arbitrary", "arbitrary")),
        name="attn_nbr",
    )(rpb.reshape(N_HEADS_A, n_bias), rest, k, v, rest, k_ctx, v_ctx)


SHORT_CHUNK = 512


def _outproj_even_kernel(oa_ref, bg_ref, cg_ref, xin_ref, gb_ref, cgp_ref, xinp_ref, cgn_ref, xinn_ref,
                         cw_ref, w_ref, x_ref, gate_ref, lng_ref, lnb_ref, o_ref, *, tm, seq):
    i = pl.program_id(0)
    row = lax.broadcasted_iota(jnp.int32, (tm, SHORT_CHUNK), 0)
    pos = (i * tm + row) % seq
    parts = [oa_ref[...]]
    for c in range(E_B // SHORT_CHUNK):
        ln = slice(c * SHORT_CHUNK, (c + 1) * SHORT_CHUNK)
        z = cg_ref[0, :, ln] * xin_ref[0, :, ln]
        z_before = cgp_ref[0, SUBLANES - 1:SUBLANES, ln] * xinp_ref[0, SUBLANES - 1:SUBLANES, ln]
        z_after = cgn_ref[0, 0:1, ln] * xinn_ref[0, 0:1, ln]
        z_prev = jnp.where(row == 0, z_before, pltpu.roll(z, 1, axis=0))
        z_prev = jnp.where(pos == 0, 0.0, z_prev)
        z_next = jnp.where(row == tm - 1, z_after, pltpu.roll(z, tm - 1, axis=0))
        z_next = jnp.where(pos == seq - 1, 0.0, z_next)
        conv = z_prev * cw_ref[0:1, ln] + z * cw_ref[1:2, ln] + z_next * cw_ref[2:3, ln]
        ob = _silu(gb_ref[0, :, ln]) * bg_ref[0, :, ln] * conv
        parts.append(ob.astype(BF16))
    lhs = jnp.concatenate(parts, axis=1)
    acc = jnp.dot(lhs, w_ref[...], preferred_element_type=F32)
    r = ALPHA * x_ref[...] + gate_ref[0] * acc
    o_ref[...] = _layernorm(r, lng_ref[...], lnb_ref[...])


def _outproj_even(oa, rest, conv_w, w_bf, x, mods, ln_g, ln_b, cond_of_tile, seq, tm):
    rows, d = x.shape
    n_sub = rows // SUBLANES
    grp = lambda g: pl.BlockSpec((1, tm, E_B), lambda i, g=g: (g, i, 0))
    before = lambda g: pl.BlockSpec(
        (1, SUBLANES, E_B), lambda i, g=g: (g, jnp.maximum(i * (tm // SUBLANES) - 1, 0), 0))
    after = lambda g: pl.BlockSpec(
        (1, SUBLANES, E_B), lambda i, g=g: (g, jnp.minimum((i + 1) * (tm // SUBLANES), n_sub - 1), 0))
    row_vec = pl.BlockSpec((1, d), lambda i: (0, 0))
    return pl.pallas_call(
        functools.partial(_outproj_even_kernel, tm=tm, seq=seq),
        grid=(rows // tm,),
        in_specs=[
            pl.BlockSpec((tm, E_A), lambda i: (i, 0)),
            grp(REST_B), grp(REST_C), grp(REST_XIN), grp(REST_GB),
            before(REST_C), before(REST_XIN), after(REST_C), after(REST_XIN),
            pl.BlockSpec((SHORT_CONV, E_B), lambda i: (0, 0)),
            _resident((E_MIX, d), lambda i: (0, 0)),
            pl.BlockSpec((tm, d), lambda i: (i, 0)),
            pl.BlockSpec((1, 1, d), lambda i: (cond_of_tile(i), 0, 2)),
            row_vec, row_vec,
        ],
        out_specs=pl.BlockSpec((tm, d), lambda i: (i, 0)),
        out_shape=jax.ShapeDtypeStruct((rows, d), F32),
        compiler_params=_params(("arbitrary",)),
        name="outproj_even",
    )(oa, rest, rest, rest, rest, rest, rest, rest, rest, conv_w, w_bf, x, mods,
      ln_g.reshape(1, d), ln_b.reshape(1, d))


CONV_GROUP = 4
ODD_PARTS = 4


def _conv_stride(seq):
    s = -(-seq // SUBLANES)
    return s if s % 2 else s + 1


def _conv_region(seq):
    rows = SUBLANES * _conv_stride(seq) + CONF_CONV
    return -(-rows // SUBLANES) * SUBLANES


def _inproj_odd_kernel(x_ref, sh_ref, sc_ref, wa_ref, wb_ref, wg_ref, cw_ref, cb_ref, sg_ref, conv_ref,
                       hb_ref, ubuf_ref, ctmp_ref, wrep_ref, *, nj, n_slab, seq, n_seq):
    i = pl.program_id(0)
    j = pl.program_id(1)
    stride = _conv_stride(seq)
    region = _conv_region(seq)
    out_rows = SUBLANES * stride

    @pl.when((i == 0) & (j == 0))
    def _():
        ubuf_ref[...] = jnp.zeros_like(ubuf_ref)

    @pl.when(j == 0)
    def _():
        hb_ref[...] = (x_ref[...] * (1.0 + sc_ref[0]) + sh_ref[0]).astype(BF16)

    tm, d = hb_ref.shape
    part_rows = tm // ODD_PARTS

    def project(part, anchor=None):
        rows = slice(part * part_rows, (part + 1) * part_rows)
        h = hb_ref[rows, :]
        if anchor is not None:
            h = h + jnp.tile(anchor, (part_rows // anchor.shape[0], d // LANES))
        a = jnp.dot(h, wa_ref[...], preferred_element_type=F32)
        b = jnp.dot(h, wb_ref[...], preferred_element_type=F32)
        sg_ref[rows, :] = _silu(jnp.dot(h, wg_ref[...], preferred_element_type=F32))
        return a * _sigmoid(b)

    def store_u(parts):
        u = jnp.concatenate(parts, axis=0)
        for s in range(n_seq):
            for c in range(n_slab):
                ubuf_ref[c, s * region + HALO:s * region + HALO + seq, :] = (
                    u[s * seq:(s + 1) * seq, c * LANES:(c + 1) * LANES])

    items = [(c, s, v0) for c in range(n_slab) for s in range(n_seq) for v0 in range(0, stride, CONV_GROUP)]
    per_part = -(-len(items) // ODD_PARTS)

    def conv_begin():
        for c in range(n_slab):
            for k in range(CONF_CONV):
                wrep_ref[c, k] = jnp.broadcast_to(cw_ref[k:k + 1, c * LANES:(c + 1) * LANES], (SUBLANES, LANES))

    def conv_part(part):
        acc = None
        for c, s, v0 in items[part * per_part:(part + 1) * per_part]:
            bias = jnp.broadcast_to(cb_ref[:, c * LANES:(c + 1) * LANES], (SUBLANES, LANES))
            vs = range(v0, min(v0 + CONV_GROUP, stride))
            accs = [bias for _ in vs]
            for k in range(CONF_CONV):
                wk = wrep_ref[c, k]
                for g, v in enumerate(vs):
                    first = s * region + HALO - CONF_HALF + v + k
                    accs[g] = accs[g] + ubuf_ref[c, pl.ds(first, SUBLANES, stride=stride), :] * wk
            for g, v in enumerate(vs):
                ctmp_ref[c, pl.ds(s * out_rows + v, SUBLANES, stride=stride), :] = accs[g]
            acc = accs[-1]
        bits = lax.bitcast_convert_type(acc, jnp.uint32)
        zero = lax.bitcast_convert_type((bits >> 16) >> 16, F32)
        return jnp.concatenate([zero, zero], axis=0).astype(BF16)

    def conv_end():
        for c in range(n_slab):
            for s in range(n_seq):
                conv_ref[c, s * seq:(s + 1) * seq, :] = ctmp_ref[c, s * out_rows:s * out_rows + seq, :]

    @pl.when(j == 0)
    def _():
        store_u([project(part) for part in range(ODD_PARTS)])

    @pl.when((j > 0) & (j < nj))
    def _():
        conv_begin()
        parts, anchor = [], None
        for part in range(ODD_PARTS):
            parts.append(project(part, anchor))
            anchor = conv_part(part)
        conv_end()
        store_u(parts)

    @pl.when(j == nj)
    def _():
        conv_begin()
        for part in range(ODD_PARTS):
            conv_part(part)
        conv_end()


def _inproj_odd(x, mods, w_bf, conv_w, conv_b, cond_of_tile, seq, tm, tn):
    rows, d = x.shape
    nj = E_C // tn
    n_slab = tn // LANES
    n_seq = tm // seq
    assert tm % seq == 0
    cur = lambda j: jnp.minimum(j, nj - 1)
    prev = lambda j: jnp.maximum(j - 1, 0)
    wspec = lambda part: pl.BlockSpec((d, tn), lambda i, j, part=part: (0, part * nj + cur(j)))
    return pl.pallas_call(
        functools.partial(_inproj_odd_kernel, nj=nj, n_slab=n_slab, seq=seq, n_seq=n_seq),
        grid=(rows // tm, nj + 1),
        in_specs=[
            pl.BlockSpec((tm, d), lambda i, j: (i, 0)),
            pl.BlockSpec((1, 1, d), lambda i, j: (cond_of_tile(i), 0, 0)),
            pl.BlockSpec((1, 1, d), lambda i, j: (cond_of_tile(i), 0, 1)),
            wspec(0), wspec(1), wspec(2),
            pl.BlockSpec((CONF_CONV, tn), lambda i, j: (0, prev(j))),
            pl.BlockSpec((1, tn), lambda i, j: (0, prev(j))),
        ],
        out_specs=[
            pl.BlockSpec((tm, tn), lambda i, j: (i, cur(j))),
            pl.BlockSpec((n_slab, tm, LANES), lambda i, j: (prev(j), i, 0)),
        ],
        out_shape=[
            jax.ShapeDtypeStruct((rows, E_C), F32),
            jax.ShapeDtypeStruct((E_C // LANES, rows, LANES), F32),
        ],
        scratch_shapes=[
            pltpu.VMEM((tm, d), BF16),
            pltpu.VMEM((n_slab, n_seq * _conv_region(seq), LANES), F32),
            pltpu.VMEM((n_slab, n_seq * SUBLANES * _conv_stride(seq), LANES), F32),
            pltpu.VMEM((n_slab, CONF_CONV, SUBLANES, LANES), F32),
        ],
        compiler_params=_params(("arbitrary", "arbitrary")),
        name="inproj_odd",
    )(x, mods, mods, w_bf, w_bf, w_bf, conv_w, conv_b.reshape(1, E_C))


def _outproj_odd_kernel(conv_ref, sg_ref, cng_ref, cnb_ref, w_ref, x_ref, gate_ref, lng_ref, lnb_ref, o_ref,
                        *, n_slab):
    total = conv_ref[0]
    for c in range(1, n_slab):
        total = total + conv_ref[c]
    mu = jnp.sum(total, axis=-1, keepdims=True) * (1.0 / E_C)
    sq = None
    for c in range(n_slab):
        dlt = conv_ref[c] - mu
        sq = dlt * dlt if sq is None else sq + dlt * dlt
    rstd = lax.rsqrt(jnp.sum(sq, axis=-1, keepdims=True) * (1.0 / E_C) + LN_EPS)
    parts = []
    for c in range(n_slab):
        ln = slice(c * LANES, (c + 1) * LANES)
        v = (conv_ref[c] - mu) * rstd * cng_ref[:, ln] + cnb_ref[:, ln]
        parts.append((_silu(v) * sg_ref[:, ln]).astype(BF16))
    lhs = jnp.concatenate(parts, axis=1)
    acc = jnp.dot(lhs, w_ref[...], preferred_element_type=F32)
    r = ALPHA * x_ref[...] + gate_ref[0] * acc
    o_ref[...] = _layernorm(r, lng_ref[...], lnb_ref[...])


def _outproj_odd(conv, sg, cn_g, cn_b, w_bf, x, mods, ln_g, ln_b, cond_of_tile, tm):
    rows, d = x.shape
    n_slab = E_C // LANES
    row_vec = pl.BlockSpec((1, d), lambda i: (0, 0))
    chan_vec = pl.BlockSpec((1, E_C), lambda i: (0, 0))
    return pl.pallas_call(
        functools.partial(_outproj_odd_kernel, n_slab=n_slab),
        grid=(rows // tm,),
        in_specs=[
            pl.BlockSpec((n_slab, tm, LANES), lambda i: (0, i, 0)),
            pl.BlockSpec((tm, E_C), lambda i: (i, 0)),
            chan_vec, chan_vec,
            _resident((E_C, d), lambda i: (0, 0)),
            pl.BlockSpec((tm, d), lambda i: (i, 0)),
            pl.BlockSpec((1, 1, d), lambda i: (cond_of_tile(i), 0, 2)),
            row_vec, row_vec,
        ],
        out_specs=pl.BlockSpec((tm, d), lambda i: (i, 0)),
        out_shape=jax.ShapeDtypeStruct((rows, d), F32),
        compiler_params=_params(("arbitrary",)),
        name="outproj_odd",
    )(conv, sg, cn_g.reshape(1, E_C), cn_b.reshape(1, E_C), w_bf, x, mods,
      ln_g.reshape(1, d), ln_b.reshape(1, d))


IN_ROWS = 1024
OUT_ROWS = 256
EVEN_COLS = 512
ODD_COLS = 256


def kernel(x_prompt, x_sample, cache_k, cache_v, c, c_ctx, w_ada, b_ada, ln_g, ln_b, w_in_even, conv_short_w,
           rpb, w_out_even, w_in_odd, conv_conf_w, conv_conf_b, ln_conf_g, ln_conf_b, w_out_odd):
    batch, seq, d = x_prompt.shape
    dec_batch, dec_seq, _ = x_sample.shape
    n_even, past = cache_k.shape[1], cache_k.shape[2]
    assert 1 + dec_batch <= N_COND and d == D_MODEL

    cond = jnp.zeros((N_COND, d), F32).at[0].set(c_ctx).at[1:1 + dec_batch].set(c)
    mods_all = _ada(cond, w_ada, b_ada)
    k_ctx = cache_k.reshape(dec_batch, n_even * past, E_A)
    v_ctx = cache_v.reshape(dec_batch, n_even * past, E_A)

    def cond_fn(is_prompt, tm):
        return (lambda i: 0) if is_prompt else (lambda i: 1 + (i * tm) // dec_seq)

    y_p = x_prompt.reshape(batch * seq, d)
    y_s = x_sample.reshape(dec_batch * dec_seq, d)
    new_k, new_v = [], []
    for layer in range(DEPTH):
        mods = mods_all[layer].reshape(N_COND, 1, 3 * d)
        outs = []
        for y, sq, is_prompt in ((y_p, seq, True), (y_s, dec_seq, False)):
            rows = y.shape[0]
            tm_in = min(rows, IN_ROWS)
            tm_out = min(rows, OUT_ROWS)
            if layer % 2 == 0:
                e = layer // 2
                rest, k, v = _inproj_even(y, mods, w_in_even[e].astype(BF16), cond_fn(is_prompt, tm_in),
                                          tm_in, EVEN_COLS)
                if is_prompt:
                    oa = _attn_ctx(rest, k, v, sq)
                    new_k.append(k.reshape(batch, seq, N_HEADS_A, HEAD_DIM))
                    new_v.append(v.reshape(batch, seq, N_HEADS_A, HEAD_DIM))
                else:
                    oa = _attn_nbr(rest, k, v, k_ctx, v_ctx, rpb[e], sq, past, e)
                outs.append(_outproj_even(oa, rest, conv_short_w[e], w_out_even[e].astype(BF16), y, mods,
                                          ln_g[layer], ln_b[layer], cond_fn(is_prompt, tm_out), sq, tm_out))
            else:
                o = layer // 2
                sg, conv = _inproj_odd(y, mods, w_in_odd[o].astype(BF16), conv_conf_w[o], conv_conf_b[o],
                                       cond_fn(is_prompt, tm_in), sq, tm_in, ODD_COLS)
                outs.append(_outproj_odd(conv, sg, ln_conf_g[o], ln_conf_b[o], w_out_odd[o].astype(BF16), y, mods,
                                         ln_g[layer], ln_b[layer], cond_fn(is_prompt, tm_out), tm_out))
        y_p, y_s = outs
    return (y_p.reshape(batch, seq, d), y_s.reshape(dec_batch, dec_seq, d),
            jnp.stack(new_k, axis=1), jnp.stack(new_v, axis=1))
```

```python
import functools

import jax
import jax.numpy as jnp
from jax import lax
from jax.experimental import pallas as pl
from jax.experimental.pallas import tpu as pltpu

F32 = jnp.float32
BF16 = jnp.bfloat16

D_MODEL = 2048
DEPTH = 2
GRID_W = 64
E_MIX = 2 * D_MODEL
E_A = E_MIX // 2
E_B = E_MIX - E_A
HEAD_DIM = 128
N_HEADS_A = E_A // HEAD_DIM
NA_ROWS = 8
NA_COLS = 16
SHORT_CONV = 3
E_C = E_MIX
CONF_CONV = 31
CONF_HALF = CONF_CONV // 2
ALPHA = (2.0 * DEPTH) ** 0.25
LN_EPS = 1e-5
NEG_BIG = -1e30
ATTN_SCALE = HEAD_DIM ** -0.5

N_COND = 8
SUBLANES = 8
LANES = 128
MXU_DEPTH = 256
HALO = 16
VMEM_LIMIT = 56 * 1024 * 1024

Q_CHUNK_ROWS = 4
KEY_WIN_ROWS = 12


def _params(semantics):
    return pltpu.CompilerParams(dimension_semantics=semantics, vmem_limit_bytes=VMEM_LIMIT)


def _resident(block_shape, index_map):
    return pl.BlockSpec(block_shape, index_map, pipeline_mode=pl.Buffered(1))


def _sigmoid(x):
    return 0.5 * jnp.tanh(0.5 * x) + 0.5


def _silu(x):
    return x * _sigmoid(x)


def _layernorm(r, g, b):
    mu = jnp.mean(r, axis=-1, keepdims=True)
    d = r - mu
    var = jnp.mean(d * d, axis=-1, keepdims=True)
    return d * lax.rsqrt(var + LN_EPS) * g + b


def _ada_kernel(cond_ref, w_ref, b_ref, o_ref):
    s = _silu(cond_ref[...]).astype(BF16)
    o_ref[0] = jnp.dot(s, w_ref[0].astype(BF16), preferred_element_type=F32) + b_ref[0]


def _ada(cond, w_ada, b_ada):
    depth, d, n3 = w_ada.shape
    tn = 1024
    return pl.pallas_call(
        _ada_kernel,
        grid=(depth, n3 // tn),
        in_specs=[
            pl.BlockSpec((N_COND, d), lambda l, j: (0, 0)),
            pl.BlockSpec((1, d, tn), lambda l, j: (l, 0, j)),
            pl.BlockSpec((1, 1, tn), lambda l, j: (l, 0, j)),
        ],
        out_specs=pl.BlockSpec((1, N_COND, tn), lambda l, j: (l, 0, j)),
        out_shape=jax.ShapeDtypeStruct((depth, N_COND, n3), F32),
        compiler_params=_params(("arbitrary", "arbitrary")),
        name="ada",
    )(cond, w_ada, b_ada.reshape(depth, 1, n3))


GROUP_K, GROUP_V = 1, 2
N_REST = 6
REST_Q, REST_GA, REST_B, REST_C, REST_XIN, REST_GB = range(N_REST)


def _inproj_even_kernel(x_ref, sh_ref, sc_ref, w_ref, rest_ref, k_ref, v_ref, hb_ref, *, npg):
    j = pl.program_id(1)

    @pl.when(j == 0)
    def _():
        hb_ref[...] = (x_ref[...] * (1.0 + sc_ref[0]) + sh_ref[0]).astype(BF16)

    group = j // npg

    @pl.when(group == GROUP_K)
    def _():
        k_ref[...] = jnp.dot(hb_ref[...], w_ref[...], preferred_element_type=F32)

    @pl.when(group == GROUP_V)
    def _():
        v_ref[...] = jnp.dot(hb_ref[...], w_ref[...], preferred_element_type=F32)

    @pl.when((group != GROUP_K) & (group != GROUP_V))
    def _():
        rest_ref[0] = jnp.dot(hb_ref[...], w_ref[...], preferred_element_type=F32)


def _inproj_even(x, mods, w_bf, cond_of_tile, tm, tn):
    rows, d = x.shape
    n = w_bf.shape[1]
    npg = E_A // tn

    def rest_map(i, j):
        group = j // npg
        slot = jnp.where(group == 0, 0, jnp.maximum(group - 2, 0))
        tile = jnp.where((group == GROUP_K) | (group == GROUP_V), npg - 1, j % npg)
        return slot, i, tile

    kv_map = lambda g: (lambda i, j: (i, jnp.clip(j - g * npg, 0, npg - 1)))
    return pl.pallas_call(
        functools.partial(_inproj_even_kernel, npg=npg),
        grid=(rows // tm, n // tn),
        in_specs=[
            pl.BlockSpec((tm, d), lambda i, j: (i, 0)),
            pl.BlockSpec((1, 1, d), lambda i, j: (cond_of_tile(i), 0, 0)),
            pl.BlockSpec((1, 1, d), lambda i, j: (cond_of_tile(i), 0, 1)),
            pl.BlockSpec((d, tn), lambda i, j: (0, j)),
        ],
        out_specs=[
            pl.BlockSpec((1, tm, tn), rest_map),
            pl.BlockSpec((tm, tn), kv_map(GROUP_K)),
            pl.BlockSpec((tm, tn), kv_map(GROUP_V)),
        ],
        out_shape=[
            jax.ShapeDtypeStruct((N_REST, rows, E_A), F32),
            jax.ShapeDtypeStruct((rows, E_A), F32),
            jax.ShapeDtypeStruct((rows, E_A), F32),
        ],
        scratch_shapes=[pltpu.VMEM((tm, d), BF16)],
        compiler_params=_params(("arbitrary", "arbitrary")),
        name="inproj_even",
    )(x, mods, mods, w_bf)


def _attn_ctx_kernel(q_ref, k_ref, v_ref, ga_ref, o_ref):
    for h in range(N_HEADS_A):
        sl = slice(h * HEAD_DIM, (h + 1) * HEAD_DIM)
        q = q_ref[0, :, sl].astype(BF16)
        k = k_ref[:, sl].astype(BF16)
        v = v_ref[:, sl].astype(BF16)
        s = lax.dot_general(q, k, (((1,), (1,)), ((), ())), preferred_element_type=F32) * ATTN_SCALE
        m = jnp.max(s, axis=-1, keepdims=True)
        p = jnp.exp(s - m)
        l = jnp.sum(p, axis=-1, keepdims=True)
        o = jnp.dot(p.astype(BF16), v, preferred_element_type=F32) / l
        o_ref[:, sl] = (o * _silu(ga_ref[0, :, sl])).astype(BF16)


def _attn_ctx(rest, k, v, seq):
    _, rows, e = rest.shape
    grp = lambda g: pl.BlockSpec((1, seq, e), lambda b, g=g: (g, b, 0))
    flat = pl.BlockSpec((seq, e), lambda b: (b, 0))
    return pl.pallas_call(
        _attn_ctx_kernel,
        grid=(rows // seq,),
        in_specs=[grp(REST_Q), flat, flat, grp(REST_GA)],
        out_specs=flat,
        out_shape=jax.ShapeDtypeStruct((rows, e), BF16),
        compiler_params=_params(("arbitrary",)),
        name="attn_ctx",
    )(rest, k, v, rest)


def _key_window_start(chunk, n_rows):
    lo = max(chunk * Q_CHUNK_ROWS - NA_ROWS // 2, 0)
    return min(lo - lo % Q_CHUNK_ROWS, n_rows - KEY_WIN_ROWS)


def _build_bias(rpb_ref, bias_ref, head, n_rows):
    pair_w = 2 * GRID_W
    qc = lax.broadcasted_iota(jnp.int32, (GRID_W, pair_w), 0)
    lane = lax.broadcasted_iota(jnp.int32, (GRID_W, pair_w), 1)
    kc = lane % GRID_W
    rel = kc - qc + (NA_COLS - 1)
    start_c = jnp.clip(qc - NA_COLS // 2, 0, GRID_W - NA_COLS)
    valid = (kc >= start_c) & (kc < start_c + NA_COLS)
    n_dc = 2 * NA_COLS - 1
    neg = jnp.full((GRID_W, pair_w), NEG_BIG, F32)
    toeplitz = []
    for i in range(2 * NA_ROWS - 1):
        t = neg
        for dcol in range(n_dc):
            t = jnp.where(rel == dcol, rpb_ref[head, i * n_dc + dcol], t)
        toeplitz.append(jnp.where(valid, t, NEG_BIG))
    for r in range(n_rows):
        w0 = _key_window_start(r // Q_CHUNK_ROWS, n_rows)
        start_r = min(max(r - NA_ROWS // 2, 0), n_rows - NA_ROWS)

        def block(kr):
            inside = start_r <= kr < start_r + NA_ROWS
            return toeplitz[kr - r + NA_ROWS - 1] if inside else neg

        for pair in range(KEY_WIN_ROWS // 2):
            blk = jnp.where(lane < GRID_W, block(w0 + 2 * pair), block(w0 + 2 * pair + 1))
            bias_ref[r * GRID_W:(r + 1) * GRID_W, pair * pair_w:(pair + 1) * pair_w] = blk


def _attn_nbr_kernel(rpb_ref, q_ref, k_ref, v_ref, ga_ref, kc_ref, vc_ref, o_ref, bias_ref, *, n_rows):
    head = pl.program_id(0)

    @pl.when(pl.program_id(1) == 0)
    def _():
        _build_bias(rpb_ref, bias_ref, head, n_rows)

    k_ctx = kc_ref[0].astype(BF16)
    v_ctx = vc_ref[0].astype(BF16)
    qn = Q_CHUNK_ROWS * GRID_W
    kn = KEY_WIN_ROWS * GRID_W
    contract_last = (((1,), (1,)), ((), ()))
    for chunk in range(n_rows // Q_CHUNK_ROWS):
        qs = slice(chunk * qn, (chunk + 1) * qn)
        k0 = _key_window_start(chunk, n_rows) * GRID_W
        q = q_ref[0, qs, :].astype(BF16)
        k_win = k_ref[k0:k0 + kn, :].astype(BF16)
        v_win = v_ref[k0:k0 + kn, :].astype(BF16)
        s_win = lax.dot_general(q, k_win, contract_last, preferred_element_type=F32) * ATTN_SCALE
        s_win = s_win + bias_ref[qs, :]
        s_ctx = lax.dot_general(q, k_ctx, contract_last, preferred_element_type=F32) * ATTN_SCALE
        m = jnp.maximum(jnp.max(s_win, axis=-1, keepdims=True), jnp.max(s_ctx, axis=-1, keepdims=True))
        p_win = jnp.exp(s_win - m)
        p_ctx = jnp.exp(s_ctx - m)
        l = jnp.sum(p_win, axis=-1, keepdims=True) + jnp.sum(p_ctx, axis=-1, keepdims=True)
        o = (jnp.dot(p_win.astype(BF16), v_win, preferred_element_type=F32)
             + jnp.dot(p_ctx.astype(BF16), v_ctx, preferred_element_type=F32)) / l
        o_ref[qs, :] = (o * _silu(ga_ref[0, qs, :])).astype(BF16)


def _attn_nbr(rest, k, v, k_ctx, v_ctx, rpb, seq, past, even_idx):
    _, rows, e = rest.shape
    n_rows = seq // GRID_W
    assert n_rows % Q_CHUNK_ROWS == 0 and n_rows >= KEY_WIN_ROWS and NA_ROWS <= n_rows
    n_bias = (2 * NA_ROWS - 1) * (2 * NA_COLS - 1)
    grp = lambda g: pl.BlockSpec((1, seq, HEAD_DIM), lambda h, b, g=g: (g, b, h))
    flat = pl.BlockSpec((seq, HEAD_DIM), lambda h, b: (b, h))
    ctx_spec = pl.BlockSpec((1, past, HEAD_DIM), lambda h, b: (b, even_idx, h))
    return pl.pallas_call(
        functools.partial(_attn_nbr_kernel, n_rows=n_rows),
        grid=(N_HEADS_A, rows // seq),
        in_specs=[
            pl.BlockSpec(memory_space=pltpu.SMEM),
            grp(REST_Q), flat, flat, grp(REST_GA), ctx_spec, ctx_spec,
        ],
        out_specs=flat,
        out_shape=jax.ShapeDtypeStruct((rows, e), BF16),
        scratch_shapes=[pltpu.VMEM((seq, KEY_WIN_ROWS * GRID_W), F32)],
        compiler_params=_params(("arbitrary", "arbitrary")),
        name="attn_nbr",
    )(rpb.reshape(N_HEADS_A, n_bias), rest, k, v, rest, k_ctx, v_ctx)


SHORT_CHUNK = 512


def _outproj_even_kernel(oa_ref, bg_ref, cg_ref, xin_ref, gb_ref, cgp_ref, xinp_ref, cgn_ref, xinn_ref,
                         cw_ref, w_ref, x_ref, gate_ref, lng_ref, lnb_ref, o_ref, *, tm, seq):
    i = pl.program_id(0)
    row = lax.broadcasted_iota(jnp.int32, (tm, SHORT_CHUNK), 0)
    pos = (i * tm + row) % seq
    parts = [oa_ref[...]]
    for c in range(E_B // SHORT_CHUNK):
        ln = slice(c * SHORT_CHUNK, (c + 1) * SHORT_CHUNK)
        z = cg_ref[0, :, ln] * xin_ref[0, :, ln]
        z_before = cgp_ref[0, SUBLANES - 1:SUBLANES, ln] * xinp_ref[0, SUBLANES - 1:SUBLANES, ln]
        z_after = cgn_ref[0, 0:1, ln] * xinn_ref[0, 0:1, ln]
        z_prev = jnp.where(row == 0, z_before, pltpu.roll(z, 1, axis=0))
        z_prev = jnp.where(pos == 0, 0.0, z_prev)
        z_next = jnp.where(row == tm - 1, z_after, pltpu.roll(z, tm - 1, axis=0))
        z_next = jnp.where(pos == seq - 1, 0.0, z_next)
        conv = z_prev * cw_ref[0:1, ln] + z * cw_ref[1:2, ln] + z_next * cw_ref[2:3, ln]
        ob = _silu(gb_ref[0, :, ln]) * bg_ref[0, :, ln] * conv
        parts.append(ob.astype(BF16))
    lhs = jnp.concatenate(parts, axis=1)
    acc = jnp.dot(lhs, w_ref[...], preferred_element_type=F32)
    r = ALPHA * x_ref[...] + gate_ref[0] * acc
    o_ref[...] = _layernorm(r, lng_ref[...], lnb_ref[...])


def _outproj_even(oa, rest, conv_w, w_bf, x, mods, ln_g, ln_b, cond_of_tile, seq, tm):
    rows, d = x.shape
    n_sub = rows // SUBLANES
    grp = lambda g: pl.BlockSpec((1, tm, E_B), lambda i, g=g: (g, i, 0))
    before = lambda g: pl.BlockSpec(
        (1, SUBLANES, E_B), lambda i, g=g: (g, jnp.maximum(i * (tm // SUBLANES) - 1, 0), 0))
    after = lambda g: pl.BlockSpec(
        (1, SUBLANES, E_B), lambda i, g=g: (g, jnp.minimum((i + 1) * (tm // SUBLANES), n_sub - 1), 0))
    row_vec = pl.BlockSpec((1, d), lambda i: (0, 0))
    return pl.pallas_call(
        functools.partial(_outproj_even_kernel, tm=tm, seq=seq),
        grid=(rows // tm,),
        in_specs=[
            pl.BlockSpec((tm, E_A), lambda i: (i, 0)),
            grp(REST_B), grp(REST_C), grp(REST_XIN), grp(REST_GB),
            before(REST_C), before(REST_XIN), after(REST_C), after(REST_XIN),
            pl.BlockSpec((SHORT_CONV, E_B), lambda i: (0, 0)),
            _resident((E_MIX, d), lambda i: (0, 0)),
            pl.BlockSpec((tm, d), lambda i: (i, 0)),
            pl.BlockSpec((1, 1, d), lambda i: (cond_of_tile(i), 0, 2)),
            row_vec, row_vec,
        ],
        out_specs=pl.BlockSpec((tm, d), lambda i: (i, 0)),
        out_shape=jax.ShapeDtypeStruct((rows, d), F32),
        compiler_params=_params(("arbitrary",)),
        name="outproj_even",
    )(oa, rest, rest, rest, rest, rest, rest, rest, rest, conv_w, w_bf, x, mods,
      ln_g.reshape(1, d), ln_b.reshape(1, d))


CONV_GROUP = 4
ODD_PARTS = 4


def _conv_stride(seq):
    s = -(-seq // SUBLANES)
    return s if s % 2 else s + 1


def _conv_region(seq):
    rows = SUBLANES * _conv_stride(seq) + CONF_CONV
    return -(-rows // SUBLANES) * SUBLANES


def _inproj_odd_kernel(x_ref, sh_ref, sc_ref, wa_ref, wb_ref, wg_ref, cw_ref, cb_ref, sg_ref, conv_ref,
                       hb_ref, ubuf_ref, ctmp_ref, wrep_ref, *, nj, n_slab, seq, n_seq):
    i = pl.program_id(0)
    j = pl.program_id(1)
    stride = _conv_stride(seq)
    region = _conv_region(seq)
    out_rows = SUBLANES * stride

    @pl.when((i == 0) & (j == 0))
    def _():
        ubuf_ref[...] = jnp.zeros_like(ubuf_ref)

    @pl.when(j == 0)
    def _():
        hb_ref[...] = (x_ref[...] * (1.0 + sc_ref[0]) + sh_ref[0]).astype(BF16)

    tm, d = hb_ref.shape
    part_rows = tm // ODD_PARTS

    def project(part):
        rows = slice(part * part_rows, (part + 1) * part_rows)
        h = hb_ref[rows, :]
        a = jnp.dot(h, wa_ref[...], preferred_element_type=F32)
        b = jnp.dot(h, wb_ref[...], preferred_element_type=F32)
        sg_ref[rows, :] = _silu(jnp.dot(h, wg_ref[...], preferred_element_type=F32))
        return a * _sigmoid(b)

    n_ktiles = d // MXU_DEPTH
    n_pieces = 3 * n_ktiles

    def paced_weights(w_ref, first_piece):
        tiles = []
        for kt in range(n_ktiles):
            w = w_ref[kt * MXU_DEPTH:(kt + 1) * MXU_DEPTH, :]
            zero = conv_piece(first_piece + kt)
            if zero is not None:
                w = w + jnp.tile(zero, (MXU_DEPTH // zero.shape[0], w.shape[1] // LANES))
            tiles.append(w)
        return jnp.concatenate(tiles, axis=0)

    def store_u(parts):
        u = jnp.concatenate(parts, axis=0)
        for s in range(n_seq):
            for c in range(n_slab):
                ubuf_ref[c, s * region + HALO:s * region + HALO + seq, :] = (
                    u[s * seq:(s + 1) * seq, c * LANES:(c + 1) * LANES])

    items = [(c, s, v0) for c in range(n_slab) for s in range(n_seq) for v0 in range(0, stride, CONV_GROUP)]
    per_piece = -(-len(items) // n_pieces)

    def conv_begin():
        for c in range(n_slab):
            for k in range(CONF_CONV):
                wrep_ref[c, k] = jnp.broadcast_to(cw_ref[k:k + 1, c * LANES:(c + 1) * LANES], (SUBLANES, LANES))

    def conv_piece(piece):
        bits = None
        for c, s, v0 in items[piece * per_piece:(piece + 1) * per_piece]:
            bias = jnp.broadcast_to(cb_ref[:, c * LANES:(c + 1) * LANES], (SUBLANES, LANES))
            vs = range(v0, min(v0 + CONV_GROUP, stride))
            accs = [bias for _ in vs]
            for k in range(CONF_CONV):
                wk = wrep_ref[c, k]
                for g, v in enumerate(vs):
                    first = s * region + HALO - CONF_HALF + v + k
                    accs[g] = accs[g] + ubuf_ref[c, pl.ds(first, SUBLANES, stride=stride), :] * wk
            for g, v in enumerate(vs):
                ctmp_ref[c, pl.ds(s * out_rows + v, SUBLANES, stride=stride), :] = accs[g]
                word = lax.bitcast_convert_type(accs[g], jnp.uint32)
                bits = word if bits is None else bits | word
        if bits is None:
            return None
        zero = lax.bitcast_convert_type((bits >> 16) >> 16, F32)
        return jnp.concatenate([zero, zero], axis=0).astype(BF16)

    def conv_end():
        for c in range(n_slab):
            for s in range(n_seq):
                conv_ref[c, s * seq:(s + 1) * seq, :] = ctmp_ref[c, s * out_rows:s * out_rows + seq, :]

    @pl.when(j == 0)
    def _():
        store_u([project(part) for part in range(ODD_PARTS)])

    @pl.when((j > 0) & (j < nj))
    def _():
        conv_begin()
        h = hb_ref[...]
        a = jnp.dot(h, paced_weights(wa_ref, 0), preferred_element_type=F32)
        b = jnp.dot(h, paced_weights(wb_ref, n_ktiles), preferred_element_type=F32)
        sg_ref[...] = _silu(jnp.dot(h, paced_weights(wg_ref, 2 * n_ktiles), preferred_element_type=F32))
        conv_end()
        store_u([a * _sigmoid(b)])

    @pl.when(j == nj)
    def _():
        conv_begin()
        for piece in range(n_pieces):
            conv_piece(piece)
        conv_end()


def _inproj_odd(x, mods, w_bf, conv_w, conv_b, cond_of_tile, seq, tm, tn):
    rows, d = x.shape
    nj = E_C // tn
    n_slab = tn // LANES
    n_seq = tm // seq
    assert tm % seq == 0
    cur = lambda j: jnp.minimum(j, nj - 1)
    prev = lambda j: jnp.maximum(j - 1, 0)
    wspec = lambda part: pl.BlockSpec((d, tn), lambda i, j, part=part: (0, part * nj + cur(j)))
    return pl.pallas_call(
        functools.partial(_inproj_odd_kernel, nj=nj, n_slab=n_slab, seq=seq, n_seq=n_seq),
        grid=(rows // tm, nj + 1),
        in_specs=[
            pl.BlockSpec((tm, d), lambda i, j: (i, 0)),
            pl.BlockSpec((1, 1, d), lambda i, j: (cond_of_tile(i), 0, 0)),
            pl.BlockSpec((1, 1, d), lambda i, j: (cond_of_tile(i), 0, 1)),
            wspec(0), wspec(1), wspec(2),
            pl.BlockSpec((CONF_CONV, tn), lambda i, j: (0, prev(j))),
            pl.BlockSpec((1, tn), lambda i, j: (0, prev(j))),
        ],
        out_specs=[
            pl.BlockSpec((tm, tn), lambda i, j: (i, cur(j))),
            pl.BlockSpec((n_slab, tm, LANES), lambda i, j: (prev(j), i, 0)),
        ],
        out_shape=[
            jax.ShapeDtypeStruct((rows, E_C), F32),
            jax.ShapeDtypeStruct((E_C // LANES, rows, LANES), F32),
        ],
        scratch_shapes=[
            pltpu.VMEM((tm, d), BF16),
            pltpu.VMEM((n_slab, n_seq * _conv_region(seq), LANES), F32),
            pltpu.VMEM((n_slab, n_seq * SUBLANES * _conv_stride(seq), LANES), F32),
            pltpu.VMEM((n_slab, CONF_CONV, SUBLANES, LANES), F32),
        ],
        compiler_params=_params(("arbitrary", "arbitrary")),
        name="inproj_odd",
    )(x, mods, mods, w_bf, w_bf, w_bf, conv_w, conv_b.reshape(1, E_C))


def _outproj_odd_kernel(conv_ref, sg_ref, cng_ref, cnb_ref, w_ref, x_ref, gate_ref, lng_ref, lnb_ref, o_ref,
                        *, n_slab):
    total = conv_ref[0]
    for c in range(1, n_slab):
        total = total + conv_ref[c]
    mu = jnp.sum(total, axis=-1, keepdims=True) * (1.0 / E_C)
    sq = None
    for c in range(n_slab):
        dlt = conv_ref[c] - mu
        sq = dlt * dlt if sq is None else sq + dlt * dlt
    rstd = lax.rsqrt(jnp.sum(sq, axis=-1, keepdims=True) * (1.0 / E_C) + LN_EPS)
    parts = []
    for c in range(n_slab):
        ln = slice(c * LANES, (c + 1) * LANES)
        v = (conv_ref[c] - mu) * rstd * cng_ref[:, ln] + cnb_ref[:, ln]
        parts.append((_silu(v) * sg_ref[:, ln]).astype(BF16))
    lhs = jnp.concatenate(parts, axis=1)
    acc = jnp.dot(lhs, w_ref[...], preferred_element_type=F32)
    r = ALPHA * x_ref[...] + gate_ref[0] * acc
    o_ref[...] = _layernorm(r, lng_ref[...], lnb_ref[...])


def _outproj_odd(conv, sg, cn_g, cn_b, w_bf, x, mods, ln_g, ln_b, cond_of_tile, tm):
    rows, d = x.shape
    n_slab = E_C // LANES
    row_vec = pl.BlockSpec((1, d), lambda i: (0, 0))
    chan_vec = pl.BlockSpec((1, E_C), lambda i: (0, 0))
    return pl.pallas_call(
        functools.partial(_outproj_odd_kernel, n_slab=n_slab),
        grid=(rows // tm,),
        in_specs=[
            pl.BlockSpec((n_slab, tm, LANES), lambda i: (0, i, 0)),
            pl.BlockSpec((tm, E_C), lambda i: (i, 0)),
            chan_vec, chan_vec,
            _resident((E_C, d), lambda i: (0, 0)),
            pl.BlockSpec((tm, d), lambda i: (i, 0)),
            pl.BlockSpec((1, 1, d), lambda i: (cond_of_tile(i), 0, 2)),
            row_vec, row_vec,
        ],
        out_specs=pl.BlockSpec((tm, d), lambda i: (i, 0)),
        out_shape=jax.ShapeDtypeStruct((rows, d), F32),
        compiler_params=_params(("arbitrary",)),
        name="outproj_odd",
    )(conv, sg, cn_g.reshape(1, E_C), cn_b.reshape(1, E_C), w_bf, x, mods,
      ln_g.reshape(1, d), ln_b.reshape(1, d))


IN_ROWS = 1024
OUT_ROWS = 256
EVEN_COLS = 512
ODD_COLS = 256


def kernel(x_prompt, x_sample, cache_k, cache_v, c, c_ctx, w_ada, b_ada, ln_g, ln_b, w_in_even, conv_short_w,
           rpb, w_out_even, w_in_odd, conv_conf_w, conv_conf_b, ln_conf_g, ln_conf_b, w_out_odd):
    batch, seq, d = x_prompt.shape
    dec_batch, dec_seq, _ = x_sample.shape
    n_even, past = cache_k.shape[1], cache_k.shape[2]
    assert 1 + dec_batch <= N_COND and d == D_MODEL

    cond = jnp.zeros((N_COND, d), F32).at[0].set(c_ctx).at[1:1 + dec_batch].set(c)
    mods_all = _ada(cond, w_ada, b_ada)
    k_ctx = cache_k.reshape(dec_batch, n_even * past, E_A)
    v_ctx = cache_v.reshape(dec_batch, n_even * past, E_A)

    def cond_fn(is_prompt, tm):
        return (lambda i: 0) if is_prompt else (lambda i: 1 + (i * tm) // dec_seq)

    y_p = x_prompt.reshape(batch * seq, d)
    y_s = x_sample.reshape(dec_batch * dec_seq, d)
    new_k, new_v = [], []
    for layer in range(DEPTH):
        mods = mods_all[layer].reshape(N_COND, 1, 3 * d)
        outs = []
        for y, sq, is_prompt in ((y_p, seq, True), (y_s, dec_seq, False)):
            rows = y.shape[0]
            tm_in = min(rows, IN_ROWS)
            tm_out = min(rows, OUT_ROWS)
            if layer % 2 == 0:
                e = layer // 2
                rest, k, v = _inproj_even(y, mods, w_in_even[e].astype(BF16), cond_fn(is_prompt, tm_in),
                                          tm_in, EVEN_COLS)
                if is_prompt:
                    oa = _attn_ctx(rest, k, v, sq)
                    new_k.append(k.reshape(batch, seq, N_HEADS_A, HEAD_DIM))
                    new_v.append(v.reshape(batch, seq, N_HEADS_A, HEAD_DIM))
                else:
                    oa = _attn_nbr(rest, k, v, k_ctx, v_ctx, rpb[e], sq, past, e)
                outs.append(_outproj_even(oa, rest, conv_short_w[e], w_out_even[e].astype(BF16), y, mods,
                                          ln_g[layer], ln_b[layer], cond_fn(is_prompt, tm_out), sq, tm_out))
            else:
                o = layer // 2
                sg, conv = _inproj_odd(y, mods, w_in_odd[o].astype(BF16), conv_conf_w[o], conv_conf_b[o],
                                       cond_fn(is_prompt, tm_in), sq, tm_in, ODD_COLS)
                outs.append(_outproj_odd(conv, sg, ln_conf_g[o], ln_conf_b[o], w_out_odd[o].astype(BF16), y, mods,
                                         ln_g[layer], ln_b[layer], cond_fn(is_prompt, tm_out), tm_out))
        y_p, y_s = outs
    return (y_p.reshape(batch, seq, d), y_s.reshape(dec_batch, dec_seq, d),
            jnp.stack(new_k, axis=1), jnp.stack(new_v, axis=1))
```

```python
import functools

import jax
import jax.numpy as jnp
from jax import lax
from jax.experimental import pallas as pl
from jax.experimental.pallas import tpu as pltpu

F32 = jnp.float32
BF16 = jnp.bfloat16

D_MODEL = 2048
DEPTH = 2
GRID_W = 64
E_MIX = 2 * D_MODEL
E_A = E_MIX // 2
E_B = E_MIX - E_A
HEAD_DIM = 128
N_HEADS_A = E_A // HEAD_DIM
NA_ROWS = 8
NA_COLS = 16
SHORT_CONV = 3
E_C = E_MIX
CONF_CONV = 31
CONF_HALF = CONF_CONV // 2
ALPHA = (2.0 * DEPTH) ** 0.25
LN_EPS = 1e-5
NEG_BIG = -1e30
ATTN_SCALE = HEAD_DIM ** -0.5

N_COND = 8
SUBLANES = 8
LANES = 128
MXU_DEPTH = 256
HALO = 16
VMEM_LIMIT = 56 * 1024 * 1024

Q_CHUNK_ROWS = 4
KEY_WIN_ROWS = 12


def _params(semantics):
    return pltpu.CompilerParams(dimension_semantics=semantics, vmem_limit_bytes=VMEM_LIMIT)


def _resident(block_shape, index_map):
    return pl.BlockSpec(block_shape, index_map, pipeline_mode=pl.Buffered(1))


def _sigmoid(x):
    return 0.5 * jnp.tanh(0.5 * x) + 0.5


def _silu(x):
    return x * _sigmoid(x)


def _layernorm(r, g, b):
    mu = jnp.mean(r, axis=-1, keepdims=True)
    d = r - mu
    var = jnp.mean(d * d, axis=-1, keepdims=True)
    return d * lax.rsqrt(var + LN_EPS) * g + b


def _ada_kernel(cond_ref, w_ref, b_ref, o_ref):
    s = _silu(cond_ref[...]).astype(BF16)
    o_ref[0] = jnp.dot(s, w_ref[0].astype(BF16), preferred_element_type=F32) + b_ref[0]


def _ada(cond, w_ada, b_ada):
    depth, d, n3 = w_ada.shape
    tn = 1024
    return pl.pallas_call(
        _ada_kernel,
        grid=(depth, n3 // tn),
        in_specs=[
            pl.BlockSpec((N_COND, d), lambda l, j: (0, 0)),
            pl.BlockSpec((1, d, tn), lambda l, j: (l, 0, j)),
            pl.BlockSpec((1, 1, tn), lambda l, j: (l, 0, j)),
        ],
        out_specs=pl.BlockSpec((1, N_COND, tn), lambda l, j: (l, 0, j)),
        out_shape=jax.ShapeDtypeStruct((depth, N_COND, n3), F32),
        compiler_params=_params(("arbitrary", "arbitrary")),
        name="ada",
    )(cond, w_ada, b_ada.reshape(depth, 1, n3))


GROUP_Q, GROUP_K, GROUP_V = 0, 1, 2
N_REST = 5
REST_GA, REST_B, REST_C, REST_XIN, REST_GB = range(N_REST)
BF16_ROWS = 16


def _inproj_even_kernel(*refs, npg, n_casts):
    x_ref, sh_ref, sc_ref, w_ref = refs[:4]
    cast_src = refs[4:4 + n_casts]
    rest_ref, k_ref, v_ref, q16_ref, k16_ref, v16_ref = refs[4 + n_casts:10 + n_casts]
    cast_dst = refs[10 + n_casts:10 + 2 * n_casts]
    hb_ref = refs[-1]
    j = pl.program_id(1)

    @pl.when(j == 0)
    def _():
        hb_ref[...] = (x_ref[...] * (1.0 + sc_ref[0]) + sh_ref[0]).astype(BF16)

    for src, dst in zip(cast_src, cast_dst):
        dst[...] = src[...].astype(BF16)

    group = j // npg

    @pl.when(group == GROUP_Q)
    def _():
        q16_ref[...] = jnp.dot(hb_ref[...], w_ref[...], preferred_element_type=F32).astype(BF16)

    @pl.when(group == GROUP_K)
    def _():
        r = jnp.dot(hb_ref[...], w_ref[...], preferred_element_type=F32)
        k_ref[...] = r
        k16_ref[...] = r.astype(BF16)

    @pl.when(group == GROUP_V)
    def _():
        r = jnp.dot(hb_ref[...], w_ref[...], preferred_element_type=F32)
        v_ref[...] = r
        v16_ref[...] = r.astype(BF16)

    @pl.when(group > GROUP_V)
    def _():
        rest_ref[0] = jnp.dot(hb_ref[...], w_ref[...], preferred_element_type=F32)


def _inproj_even(x, mods, w_bf, cond_of_tile, tm, tn, casts):
    rows, d = x.shape
    n = w_bf.shape[1]
    npg = E_A // tn
    nj = n // tn
    n_steps = (rows // tm) * nj

    def rest_map(i, j):
        group = j // npg
        return jnp.maximum(group - 3, 0), i, jnp.where(group >= 3, j % npg, 0)

    col_map = lambda g: (lambda i, j: (i, jnp.clip(j - g * npg, 0, npg - 1)))

    def cast_spec(a):
        r, c = a.shape
        rb = max(BF16_ROWS, r // n_steps)
        assert r % rb == 0 and n_steps % (r // rb) == 0
        rep = n_steps // (r // rb)
        return pl.BlockSpec((rb, c), lambda i, j: ((i * nj + j) // rep, 0))

    f32_tile = lambda g: pl.BlockSpec((tm, tn), col_map(g))
    return pl.pallas_call(
        functools.partial(_inproj_even_kernel, npg=npg, n_casts=len(casts)),
        grid=(rows // tm, nj),
        in_specs=[
            pl.BlockSpec((tm, d), lambda i, j: (i, 0)),
            pl.BlockSpec((1, 1, d), lambda i, j: (cond_of_tile(i), 0, 0)),
            pl.BlockSpec((1, 1, d), lambda i, j: (cond_of_tile(i), 0, 1)),
            pl.BlockSpec((d, tn), lambda i, j: (0, j)),
        ] + [cast_spec(a) for a in casts],
        out_specs=[
            pl.BlockSpec((1, tm, tn), rest_map),
            f32_tile(GROUP_K), f32_tile(GROUP_V),
            f32_tile(GROUP_Q), f32_tile(GROUP_K), f32_tile(GROUP_V),
        ] + [cast_spec(a) for a in casts],
        out_shape=[
            jax.ShapeDtypeStruct((N_REST, rows, E_A), F32),
            jax.ShapeDtypeStruct((rows, E_A), F32),
            jax.ShapeDtypeStruct((rows, E_A), F32),
            jax.ShapeDtypeStruct((rows, E_A), BF16),
            jax.ShapeDtypeStruct((rows, E_A), BF16),
            jax.ShapeDtypeStruct((rows, E_A), BF16),
        ] + [jax.ShapeDtypeStruct(a.shape, BF16) for a in casts],
        scratch_shapes=[pltpu.VMEM((tm, d), BF16)],
        compiler_params=_params(("arbitrary", "arbitrary")),
        name="inproj_even",
    )(x, mods, mods, w_bf, *casts)


def _attn_ctx_kernel(q_ref, k_ref, v_ref, ga_ref, o_ref):
    for h in range(N_HEADS_A):
        sl = slice(h * HEAD_DIM, (h + 1) * HEAD_DIM)
        q = q_ref[:, sl]
        k = k_ref[:, sl]
        v = v_ref[:, sl]
        s = lax.dot_general(q, k, (((1,), (1,)), ((), ())), preferred_element_type=F32) * ATTN_SCALE
        m = jnp.max(s, axis=-1, keepdims=True)
        p = jnp.exp(s - m)
        l = jnp.sum(p, axis=-1, keepdims=True)
        o = jnp.dot(p.astype(BF16), v, preferred_element_type=F32) / l
        o_ref[:, sl] = (o * _silu(ga_ref[0, :, sl])).astype(BF16)


def _attn_ctx(rest, q, k, v, seq):
    _, rows, e = rest.shape
    grp = lambda g: pl.BlockSpec((1, seq, e), lambda b, g=g: (g, b, 0))
    flat = pl.BlockSpec((seq, e), lambda b: (b, 0))
    return pl.pallas_call(
        _attn_ctx_kernel,
        grid=(rows // seq,),
        in_specs=[flat, flat, flat, grp(REST_GA)],
        out_specs=flat,
        out_shape=jax.ShapeDtypeStruct((rows, e), BF16),
        compiler_params=_params(("arbitrary",)),
        name="attn_ctx",
    )(q, k, v, rest)


def _key_window_start(chunk, n_rows):
    lo = max(chunk * Q_CHUNK_ROWS - NA_ROWS // 2, 0)
    return min(lo - lo % Q_CHUNK_ROWS, n_rows - KEY_WIN_ROWS)


def _build_bias(rpb_ref, bias_ref, head, n_rows):
    pair_w = 2 * GRID_W
    qc = lax.broadcasted_iota(jnp.int32, (GRID_W, pair_w), 0)
    lane = lax.broadcasted_iota(jnp.int32, (GRID_W, pair_w), 1)
    kc = lane % GRID_W
    rel = kc - qc + (NA_COLS - 1)
    start_c = jnp.clip(qc - NA_COLS // 2, 0, GRID_W - NA_COLS)
    valid = (kc >= start_c) & (kc < start_c + NA_COLS)
    n_dc = 2 * NA_COLS - 1
    neg = jnp.full((GRID_W, pair_w), NEG_BIG, F32)
    toeplitz = []
    for i in range(2 * NA_ROWS - 1):
        t = neg
        for dcol in range(n_dc):
            t = jnp.where(rel == dcol, rpb_ref[head, i * n_dc + dcol], t)
        toeplitz.append(jnp.where(valid, t, NEG_BIG))
    for r in range(n_rows):
        w0 = _key_window_start(r // Q_CHUNK_ROWS, n_rows)
        start_r = min(max(r - NA_ROWS // 2, 0), n_rows - NA_ROWS)

        def block(kr):
            inside = start_r <= kr < start_r + NA_ROWS
            return toeplitz[kr - r + NA_ROWS - 1] if inside else neg

        for pair in range(KEY_WIN_ROWS // 2):
            blk = jnp.where(lane < GRID_W, block(w0 + 2 * pair), block(w0 + 2 * pair + 1))
            bias_ref[r * GRID_W:(r + 1) * GRID_W, pair * pair_w:(pair + 1) * pair_w] = blk


def _attn_nbr_kernel(rpb_ref, q_ref, k_ref, v_ref, ga_ref, kc_ref, vc_ref, o_ref, bias_ref, *, n_rows):
    head = pl.program_id(0)

    @pl.when(pl.program_id(1) == 0)
    def _():
        _build_bias(rpb_ref, bias_ref, head, n_rows)

    k_ctx = kc_ref[0].astype(BF16)
    v_ctx = vc_ref[0].astype(BF16)
    qn = Q_CHUNK_ROWS * GRID_W
    kn = KEY_WIN_ROWS * GRID_W
    contract_last = (((1,), (1,)), ((), ()))
    for chunk in range(n_rows // Q_CHUNK_ROWS):
        qs = slice(chunk * qn, (chunk + 1) * qn)
        k0 = _key_window_start(chunk, n_rows) * GRID_W
        q = q_ref[qs, :]
        k_win = k_ref[k0:k0 + kn, :]
        v_win = v_ref[k0:k0 + kn, :]
        s_win = lax.dot_general(q, k_win, contract_last, preferred_element_type=F32) * ATTN_SCALE
        s_win = s_win + bias_ref[qs, :]
        s_ctx = lax.dot_general(q, k_ctx, contract_last, preferred_element_type=F32) * ATTN_SCALE
        m = jnp.maximum(jnp.max(s_win, axis=-1, keepdims=True), jnp.max(s_ctx, axis=-1, keepdims=True))
        p_win = jnp.exp(s_win - m)
        p_ctx = jnp.exp(s_ctx - m)
        l = jnp.sum(p_win, axis=-1, keepdims=True) + jnp.sum(p_ctx, axis=-1, keepdims=True)
        o = (jnp.dot(p_win.astype(BF16), v_win, preferred_element_type=F32)
             + jnp.dot(p_ctx.astype(BF16), v_ctx, preferred_element_type=F32)) / l
        o_ref[qs, :] = (o * _silu(ga_ref[0, qs, :])).astype(BF16)


def _attn_nbr(rest, q, k, v, k_ctx, v_ctx, rpb, seq, past, even_idx):
    _, rows, e = rest.shape
    n_rows = seq // GRID_W
    assert n_rows % Q_CHUNK_ROWS == 0 and n_rows >= KEY_WIN_ROWS and NA_ROWS <= n_rows
    n_bias = (2 * NA_ROWS - 1) * (2 * NA_COLS - 1)
    grp = lambda g: pl.BlockSpec((1, seq, HEAD_DIM), lambda h, b, g=g: (g, b, h))
    flat = pl.BlockSpec((seq, HEAD_DIM), lambda h, b: (b, h))
    ctx_spec = pl.BlockSpec((1, past, HEAD_DIM), lambda h, b: (b, even_idx, h))
    return pl.pallas_call(
        functools.partial(_attn_nbr_kernel, n_rows=n_rows),
        grid=(N_HEADS_A, rows // seq),
        in_specs=[
            pl.BlockSpec(memory_space=pltpu.SMEM),
            flat, flat, flat, grp(REST_GA), ctx_spec, ctx_spec,
        ],
        out_specs=flat,
        out_shape=jax.ShapeDtypeStruct((rows, e), BF16),
        scratch_shapes=[pltpu.VMEM((seq, KEY_WIN_ROWS * GRID_W), F32)],
        compiler_params=_params(("arbitrary", "arbitrary")),
        name="attn_nbr",
    )(rpb.reshape(N_HEADS_A, n_bias), q, k, v, rest, k_ctx, v_ctx)


SHORT_CHUNK = 512


def _outproj_even_kernel(oa_ref, bg_ref, cg_ref, xin_ref, gb_ref, cgp_ref, xinp_ref, cgn_ref, xinn_ref,
                         cw_ref, w_ref, x_ref, gate_ref, lng_ref, lnb_ref, o_ref, *, tm, seq):
    i = pl.program_id(0)
    row = lax.broadcasted_iota(jnp.int32, (tm, SHORT_CHUNK), 0)
    pos = (i * tm + row) % seq
    parts = [oa_ref[...]]
    for c in range(E_B // SHORT_CHUNK):
        ln = slice(c * SHORT_CHUNK, (c + 1) * SHORT_CHUNK)
        z = cg_ref[0, :, ln] * xin_ref[0, :, ln]
        z_before = cgp_ref[0, SUBLANES - 1:SUBLANES, ln] * xinp_ref[0, SUBLANES - 1:SUBLANES, ln]
        z_after = cgn_ref[0, 0:1, ln] * xinn_ref[0, 0:1, ln]
        z_prev = jnp.where(row == 0, z_before, pltpu.roll(z, 1, axis=0))
        z_prev = jnp.where(pos == 0, 0.0, z_prev)
        z_next = jnp.where(row == tm - 1, z_after, pltpu.roll(z, tm - 1, axis=0))
        z_next = jnp.where(pos == seq - 1, 0.0, z_next)
        conv = z_prev * cw_ref[0:1, ln] + z * cw_ref[1:2, ln] + z_next * cw_ref[2:3, ln]
        ob = _silu(gb_ref[0, :, ln]) * bg_ref[0, :, ln] * conv
        parts.append(ob.astype(BF16))
    lhs = jnp.concatenate(parts, axis=1)
    acc = jnp.dot(lhs, w_ref[...], preferred_element_type=F32)
    r = ALPHA * x_ref[...] + gate_ref[0] * acc
    o_ref[...] = _layernorm(r, lng_ref[...], lnb_ref[...])


def _outproj_even(oa, rest, conv_w, w_bf, x, mods, ln_g, ln_b, cond_of_tile, seq, tm):
    rows, d = x.shape
    n_sub = rows // SUBLANES
    grp = lambda g: pl.BlockSpec((1, tm, E_B), lambda i, g=g: (g, i, 0))
    before = lambda g: pl.BlockSpec(
        (1, SUBLANES, E_B), lambda i, g=g: (g, jnp.maximum(i * (tm // SUBLANES) - 1, 0), 0))
    after = lambda g: pl.BlockSpec(
        (1, SUBLANES, E_B), lambda i, g=g: (g, jnp.minimum((i + 1) * (tm // SUBLANES), n_sub - 1), 0))
    row_vec = pl.BlockSpec((1, d), lambda i: (0, 0))
    return pl.pallas_call(
        functools.partial(_outproj_even_kernel, tm=tm, seq=seq),
        grid=(rows // tm,),
        in_specs=[
            pl.BlockSpec((tm, E_A), lambda i: (i, 0)),
            grp(REST_B), grp(REST_C), grp(REST_XIN), grp(REST_GB),
            before(REST_C), before(REST_XIN), after(REST_C), after(REST_XIN),
            pl.BlockSpec((SHORT_CONV, E_B), lambda i: (0, 0)),
            _resident((E_MIX, d), lambda i: (0, 0)),
            pl.BlockSpec((tm, d), lambda i: (i, 0)),
            pl.BlockSpec((1, 1, d), lambda i: (cond_of_tile(i), 0, 2)),
            row_vec, row_vec,
        ],
        out_specs=pl.BlockSpec((tm, d), lambda i: (i, 0)),
        out_shape=jax.ShapeDtypeStruct((rows, d), F32),
        compiler_params=_params(("arbitrary",)),
        name="outproj_even",
    )(oa, rest, rest, rest, rest, rest, rest, rest, rest, conv_w, w_bf, x, mods,
      ln_g.reshape(1, d), ln_b.reshape(1, d))


CONV_GROUP = 4
ODD_PARTS = 4


def _conv_stride(seq):
    s = -(-seq // SUBLANES)
    return s if s % 2 else s + 1


def _conv_region(seq):
    rows = SUBLANES * _conv_stride(seq) + CONF_CONV
    return -(-rows // SUBLANES) * SUBLANES


def _inproj_odd_kernel(x_ref, sh_ref, sc_ref, wa_ref, wb_ref, wg_ref, cw_ref, cb_ref, sg_ref, conv_ref,
                       hb_ref, ubuf_ref, ctmp_ref, wrep_ref, *, nj, n_steps, n_slab, seq, n_seq):
    t = pl.program_id(0)
    stride = _conv_stride(seq)
    region = _conv_region(seq)
    out_rows = SUBLANES * stride

    @pl.when(t == 0)
    def _():
        ubuf_ref[...] = jnp.zeros_like(ubuf_ref)

    @pl.when((t % nj == 0) & (t < n_steps))
    def _():
        hb_ref[...] = (x_ref[...] * (1.0 + sc_ref[0]) + sh_ref[0]).astype(BF16)

    tm, d = hb_ref.shape
    part_rows = tm // ODD_PARTS

    def project(part):
        rows = slice(part * part_rows, (part + 1) * part_rows)
        h = hb_ref[rows, :]
        a = jnp.dot(h, wa_ref[...], preferred_element_type=F32)
        b = jnp.dot(h, wb_ref[...], preferred_element_type=F32)
        sg_ref[rows, :] = _silu(jnp.dot(h, wg_ref[...], preferred_element_type=F32))
        return a * _sigmoid(b)

    n_ktiles = d // MXU_DEPTH
    n_pieces = 3 * n_ktiles

    def paced_weights(w_ref, first_piece):
        tiles = []
        for kt in range(n_ktiles):
            w = w_ref[kt * MXU_DEPTH:(kt + 1) * MXU_DEPTH, :]
            zero = conv_piece(first_piece + kt)
            if zero is not None:
                w = w + jnp.tile(zero, (MXU_DEPTH // zero.shape[0], w.shape[1] // LANES))
            tiles.append(w)
        return jnp.concatenate(tiles, axis=0)

    def store_u(parts):
        u = jnp.concatenate(parts, axis=0)
        for s in range(n_seq):
            for c in range(n_slab):
                ubuf_ref[c, s * region + HALO:s * region + HALO + seq, :] = (
                    u[s * seq:(s + 1) * seq, c * LANES:(c + 1) * LANES])

    items = [(c, s, v0) for c in range(n_slab) for s in range(n_seq) for v0 in range(0, stride, CONV_GROUP)]
    per_piece = -(-len(items) // n_pieces)

    def conv_begin():
        for c in range(n_slab):
            for k in range(CONF_CONV):
                wrep_ref[c, k] = jnp.broadcast_to(cw_ref[k:k + 1, c * LANES:(c + 1) * LANES], (SUBLANES, LANES))

    def conv_piece(piece):
        bits = None
        for c, s, v0 in items[piece * per_piece:(piece + 1) * per_piece]:
            bias = jnp.broadcast_to(cb_ref[:, c * LANES:(c + 1) * LANES], (SUBLANES, LANES))
            vs = range(v0, min(v0 + CONV_GROUP, stride))
            accs = [bias for _ in vs]
            for k in range(CONF_CONV):
                wk = wrep_ref[c, k]
                for g, v in enumerate(vs):
                    first = s * region + HALO - CONF_HALF + v + k
                    accs[g] = accs[g] + ubuf_ref[c, pl.ds(first, SUBLANES, stride=stride), :] * wk
            for g, v in enumerate(vs):
                ctmp_ref[c, pl.ds(s * out_rows + v, SUBLANES, stride=stride), :] = accs[g]
                word = lax.bitcast_convert_type(accs[g], jnp.uint32)
                bits = word if bits is None else bits | word
        if bits is None:
            return None
        zero = lax.bitcast_convert_type((bits >> 16) >> 16, F32)
        return jnp.concatenate([zero, zero], axis=0).astype(BF16)

    def conv_end():
        for c in range(n_slab):
            for s in range(n_seq):
                conv_ref[c, s * seq:(s + 1) * seq, :] = ctmp_ref[c, s * out_rows:s * out_rows + seq, :]

    @pl.when(t == 0)
    def _():
        store_u([project(part) for part in range(ODD_PARTS)])

    @pl.when((t > 0) & (t < n_steps))
    def _():
        conv_begin()
        h = hb_ref[...]
        a = jnp.dot(h, paced_weights(wa_ref, 0), preferred_element_type=F32)
        b = jnp.dot(h, paced_weights(wb_ref, n_ktiles), preferred_element_type=F32)
        sg_ref[...] = _silu(jnp.dot(h, paced_weights(wg_ref, 2 * n_ktiles), preferred_element_type=F32))
        conv_end()
        store_u([a * _sigmoid(b)])

    @pl.when(t == n_steps)
    def _():
        conv_begin()
        for piece in range(n_pieces):
            conv_piece(piece)
        conv_end()


def _inproj_odd(x, mods, w_bf, conv_w, conv_b, cond_of_tile, seq, tm, tn):
    rows, d = x.shape
    nj = E_C // tn
    n_steps = (rows // tm) * nj
    n_slab = tn // LANES
    n_seq = tm // seq
    assert tm % seq == 0
    cur = lambda t: jnp.minimum(t, n_steps - 1)
    prev = lambda t: jnp.maximum(t - 1, 0)
    wspec = lambda part: pl.BlockSpec((d, tn), lambda t, part=part: (0, part * nj + cur(t) % nj))
    return pl.pallas_call(
        functools.partial(_inproj_odd_kernel, nj=nj, n_steps=n_steps, n_slab=n_slab, seq=seq, n_seq=n_seq),
        grid=(n_steps + 1,),
        in_specs=[
            pl.BlockSpec((tm, d), lambda t: (cur(t) // nj, 0)),
            pl.BlockSpec((1, 1, d), lambda t: (cond_of_tile(cur(t) // nj), 0, 0)),
            pl.BlockSpec((1, 1, d), lambda t: (cond_of_tile(cur(t) // nj), 0, 1)),
            wspec(0), wspec(1), wspec(2),
            pl.BlockSpec((CONF_CONV, tn), lambda t: (0, prev(t) % nj)),
            pl.BlockSpec((1, tn), lambda t: (0, prev(t) % nj)),
        ],
        out_specs=[
            pl.BlockSpec((tm, tn), lambda t: (cur(t) // nj, cur(t) % nj)),
            pl.BlockSpec((n_slab, tm, LANES), lambda t: (prev(t) % nj, prev(t) // nj, 0)),
        ],
        out_shape=[
            jax.ShapeDtypeStruct((rows, E_C), F32),
            jax.ShapeDtypeStruct((E_C // LANES, rows, LANES), F32),
        ],
        scratch_shapes=[
            pltpu.VMEM((tm, d), BF16),
            pltpu.VMEM((n_slab, n_seq * _conv_region(seq), LANES), F32),
            pltpu.VMEM((n_slab, n_seq * SUBLANES * _conv_stride(seq), LANES), F32),
            pltpu.VMEM((n_slab, CONF_CONV, SUBLANES, LANES), F32),
        ],
        compiler_params=_params(("arbitrary",)),
        name="inproj_odd",
    )(x, mods, mods, w_bf, w_bf, w_bf, conv_w, conv_b.reshape(1, E_C))


def _outproj_odd_kernel(conv_ref, sg_ref, cng_ref, cnb_ref, w_ref, x_ref, gate_ref, lng_ref, lnb_ref, o_ref,
                        *, n_slab):
    total = conv_ref[0]
    for c in range(1, n_slab):
        total = total + conv_ref[c]
    mu = jnp.sum(total, axis=-1, keepdims=True) * (1.0 / E_C)
    sq = None
    for c in range(n_slab):
        dlt = conv_ref[c] - mu
        sq = dlt * dlt if sq is None else sq + dlt * dlt
    rstd = lax.rsqrt(jnp.sum(sq, axis=-1, keepdims=True) * (1.0 / E_C) + LN_EPS)
    parts = []
    for c in range(n_slab):
        ln = slice(c * LANES, (c + 1) * LANES)
        v = (conv_ref[c] - mu) * rstd * cng_ref[:, ln] + cnb_ref[:, ln]
        parts.append((_silu(v) * sg_ref[:, ln]).astype(BF16))
    lhs = jnp.concatenate(parts, axis=1)
    acc = jnp.dot(lhs, w_ref[...], preferred_element_type=F32)
    r = ALPHA * x_ref[...] + gate_ref[0] * acc
    o_ref[...] = _layernorm(r, lng_ref[...], lnb_ref[...])


def _outproj_odd(conv, sg, cn_g, cn_b, w_bf, x, mods, ln_g, ln_b, cond_of_tile, tm):
    rows, d = x.shape
    n_slab = E_C // LANES
    row_vec = pl.BlockSpec((1, d), lambda i: (0, 0))
    chan_vec = pl.BlockSpec((1, E_C), lambda i: (0, 0))
    return pl.pallas_call(
        functools.partial(_outproj_odd_kernel, n_slab=n_slab),
        grid=(rows // tm,),
        in_specs=[
            pl.BlockSpec((n_slab, tm, LANES), lambda i: (0, i, 0)),
            pl.BlockSpec((tm, E_C), lambda i: (i, 0)),
            chan_vec, chan_vec,
            _resident((E_C, d), lambda i: (0, 0)),
            pl.BlockSpec((tm, d), lambda i: (i, 0)),
            pl.BlockSpec((1, 1, d), lambda i: (cond_of_tile(i), 0, 2)),
            row_vec, row_vec,
        ],
        out_specs=pl.BlockSpec((tm, d), lambda i: (i, 0)),
        out_shape=jax.ShapeDtypeStruct((rows, d), F32),
        compiler_params=_params(("arbitrary",)),
        name="outproj_odd",
    )(conv, sg, cn_g.reshape(1, E_C), cn_b.reshape(1, E_C), w_bf, x, mods,
      ln_g.reshape(1, d), ln_b.reshape(1, d))


IN_ROWS = 1024
OUT_ROWS = 256
EVEN_COLS = 512
ODD_COLS = 256


def kernel(x_prompt, x_sample, cache_k, cache_v, c, c_ctx, w_ada, b_ada, ln_g, ln_b, w_in_even, conv_short_w,
           rpb, w_out_even, w_in_odd, conv_conf_w, conv_conf_b, ln_conf_g, ln_conf_b, w_out_odd):
    batch, seq, d = x_prompt.shape
    dec_batch, dec_seq, _ = x_sample.shape
    n_even, past = cache_k.shape[1], cache_k.shape[2]
    assert 1 + dec_batch <= N_COND and d == D_MODEL

    cond = jnp.zeros((N_COND, d), F32).at[0].set(c_ctx).at[1:1 + dec_batch].set(c)
    mods_all = _ada(cond, w_ada, b_ada)
    k_ctx = cache_k.reshape(dec_batch, n_even * past, E_A)
    v_ctx = cache_v.reshape(dec_batch, n_even * past, E_A)

    def cond_fn(is_prompt, tm):
        return (lambda i: 0) if is_prompt else (lambda i: 1 + (i * tm) // dec_seq)

    y_p = x_prompt.reshape(batch * seq, d)
    y_s = x_sample.reshape(dec_batch * dec_seq, d)
    streams = lambda: ((y_p, seq, True), (y_s, dec_seq, False))
    bf16_weights = {}

    def as_bf16(name, w):
        return bf16_weights[name] if name in bf16_weights else w.astype(BF16)

    new_k, new_v = [], []
    for layer in range(DEPTH):
        mods = mods_all[layer].reshape(N_COND, 1, 3 * d)
        half = layer // 2
        has_next = layer + 1 < DEPTH
        outs = []
        if layer % 2 == 0:
            side = {True: [("in_odd", w_in_odd)] if has_next else [],
                    False: [("out_even", w_out_even)] + ([("out_odd", w_out_odd)] if has_next else [])}
            w_in = w_in_even[half].astype(BF16)
            proj = {}
            for y, sq, is_prompt in streams():
                tm_in = min(y.shape[0], IN_ROWS)
                res = _inproj_even(y, mods, w_in, cond_fn(is_prompt, tm_in), tm_in, EVEN_COLS,
                                   [w[half] for _, w in side[is_prompt]])
                proj[is_prompt] = res[:6]
                for (name, _), w16 in zip(side[is_prompt], res[6:]):
                    bf16_weights[(name, half)] = w16
            w_out = as_bf16(("out_even", half), w_out_even[half])
            for y, sq, is_prompt in streams():
                rest, k, v, q16, k16, v16 = proj[is_prompt]
                tm_out = min(y.shape[0], OUT_ROWS)
                if is_prompt:
                    oa = _attn_ctx(rest, q16, k16, v16, sq)
                    new_k.append(k.reshape(batch, seq, N_HEADS_A, HEAD_DIM))
                    new_v.append(v.reshape(batch, seq, N_HEADS_A, HEAD_DIM))
                else:
                    oa = _attn_nbr(rest, q16, k16, v16, k_ctx, v_ctx, rpb[half], sq, past, half)
                outs.append(_outproj_even(oa, rest, conv_short_w[half], w_out, y, mods,
                                          ln_g[layer], ln_b[layer], cond_fn(is_prompt, tm_out), sq, tm_out))
        else:
            w_in = as_bf16(("in_odd", half), w_in_odd[half])
            w_out = as_bf16(("out_odd", half), w_out_odd[half])
            for y, sq, is_prompt in streams():
                tm_in = min(y.shape[0], IN_ROWS)
                tm_out = min(y.shape[0], OUT_ROWS)
                sg, conv = _inproj_odd(y, mods, w_in, conv_conf_w[half], conv_conf_b[half],
                                       cond_fn(is_prompt, tm_in), sq, tm_in, ODD_COLS)
                outs.append(_outproj_odd(conv, sg, ln_conf_g[half], ln_conf_b[half], w_out, y, mods,
                                         ln_g[layer], ln_b[layer], cond_fn(is_prompt, tm_out), tm_out))
        y_p, y_s = outs
    return (y_p.reshape(batch, seq, d), y_s.reshape(dec_batch, dec_seq, d),
            jnp.stack(new_k, axis=1), jnp.stack(new_v, axis=1))
```

```python
import functools

import jax
import jax.numpy as jnp
from jax import lax
from jax.experimental import pallas as pl
from jax.experimental.pallas import tpu as pltpu

F32 = jnp.float32
BF16 = jnp.bfloat16

D_MODEL = 2048
DEPTH = 2
GRID_W = 64
E_MIX = 2 * D_MODEL
E_A = E_MIX // 2
E_B = E_MIX - E_A
HEAD_DIM = 128
N_HEADS_A = E_A // HEAD_DIM
NA_ROWS = 8
NA_COLS = 16
SHORT_CONV = 3
E_C = E_MIX
CONF_CONV = 31
CONF_HALF = CONF_CONV // 2
ALPHA = (2.0 * DEPTH) ** 0.25
LN_EPS = 1e-5
NEG_BIG = -1e30
ATTN_SCALE = HEAD_DIM ** -0.5

N_COND = 8
SUBLANES = 8
LANES = 128
MXU_DEPTH = 256
HALO = 16
VMEM_LIMIT = 56 * 1024 * 1024

Q_CHUNK_ROWS = 4
KEY_WIN_ROWS = 12


def _params(semantics):
    return pltpu.CompilerParams(dimension_semantics=semantics, vmem_limit_bytes=VMEM_LIMIT)


def _resident(block_shape, index_map):
    return pl.BlockSpec(block_shape, index_map, pipeline_mode=pl.Buffered(1))


def _sigmoid(x):
    return 0.5 * jnp.tanh(0.5 * x) + 0.5


def _silu(x):
    h = 0.5 * x
    return h + h * jnp.tanh(h)


def _layernorm(r, g, b):
    mu = jnp.mean(r, axis=-1, keepdims=True)
    d = r - mu
    var = jnp.mean(d * d, axis=-1, keepdims=True)
    return d * lax.rsqrt(var + LN_EPS) * g + b


def _ada_kernel(cond_ref, w_ref, b_ref, o_ref):
    s = _silu(cond_ref[...]).astype(BF16)
    o_ref[0] = jnp.dot(s, w_ref[0].astype(BF16), preferred_element_type=F32) + b_ref[0]


def _ada(cond, w_ada, b_ada):
    depth, d, n3 = w_ada.shape
    tn = 1024
    return pl.pallas_call(
        _ada_kernel,
        grid=(depth, n3 // tn),
        in_specs=[
            pl.BlockSpec((N_COND, d), lambda l, j: (0, 0)),
            pl.BlockSpec((1, d, tn), lambda l, j: (l, 0, j)),
            pl.BlockSpec((1, 1, tn), lambda l, j: (l, 0, j)),
        ],
        out_specs=pl.BlockSpec((1, N_COND, tn), lambda l, j: (l, 0, j)),
        out_shape=jax.ShapeDtypeStruct((depth, N_COND, n3), F32),
        compiler_params=_params(("arbitrary", "arbitrary")),
        name="ada",
    )(cond, w_ada, b_ada.reshape(depth, 1, n3))


GROUP_Q, GROUP_K, GROUP_V = 0, 1, 2
N_REST = 5
REST_GA, REST_B, REST_C, REST_XIN, REST_GB = range(N_REST)
BF16_ROWS = 16


def _inproj_even_kernel(*refs, npg, n_casts):
    x_ref, sh_ref, sc_ref, w_ref = refs[:4]
    cast_src = refs[4:4 + n_casts]
    rest_ref, k_ref, v_ref, q16_ref, k16_ref, v16_ref = refs[4 + n_casts:10 + n_casts]
    cast_dst = refs[10 + n_casts:10 + 2 * n_casts]
    hb_ref = refs[-1]
    j = pl.program_id(1)

    @pl.when(j == 0)
    def _():
        hb_ref[...] = (x_ref[...] * (1.0 + sc_ref[0]) + sh_ref[0]).astype(BF16)

    for src, dst in zip(cast_src, cast_dst):
        dst[...] = src[...].astype(BF16)

    group = j // npg

    @pl.when(group == GROUP_Q)
    def _():
        q16_ref[...] = jnp.dot(hb_ref[...], w_ref[...], preferred_element_type=F32).astype(BF16)

    @pl.when(group == GROUP_K)
    def _():
        r = jnp.dot(hb_ref[...], w_ref[...], preferred_element_type=F32)
        k_ref[...] = r
        k16_ref[...] = r.astype(BF16)

    @pl.when(group == GROUP_V)
    def _():
        r = jnp.dot(hb_ref[...], w_ref[...], preferred_element_type=F32)
        v_ref[...] = r
        v16_ref[...] = r.astype(BF16)

    @pl.when(group > GROUP_V)
    def _():
        rest_ref[0] = jnp.dot(hb_ref[...], w_ref[...], preferred_element_type=F32)


def _inproj_even(x, mods, w_bf, cond_of_tile, tm, tn, casts):
    rows, d = x.shape
    n = w_bf.shape[1]
    npg = E_A // tn
    nj = n // tn
    n_steps = (rows // tm) * nj

    def rest_map(i, j):
        group = j // npg
        return jnp.maximum(group - 3, 0), i, jnp.where(group >= 3, j % npg, 0)

    col_map = lambda g: (lambda i, j: (i, jnp.clip(j - g * npg, 0, npg - 1)))

    def cast_spec(a):
        r, c = a.shape
        rb = max(BF16_ROWS, r // n_steps)
        assert r % rb == 0 and n_steps % (r // rb) == 0
        rep = n_steps // (r // rb)
        return pl.BlockSpec((rb, c), lambda i, j: ((i * nj + j) // rep, 0))

    f32_tile = lambda g: pl.BlockSpec((tm, tn), col_map(g))
    return pl.pallas_call(
        functools.partial(_inproj_even_kernel, npg=npg, n_casts=len(casts)),
        grid=(rows // tm, nj),
        in_specs=[
            pl.BlockSpec((tm, d), lambda i, j: (i, 0)),
            pl.BlockSpec((1, 1, d), lambda i, j: (cond_of_tile(i), 0, 0)),
            pl.BlockSpec((1, 1, d), lambda i, j: (cond_of_tile(i), 0, 1)),
            pl.BlockSpec((d, tn), lambda i, j: (0, j)),
        ] + [cast_spec(a) for a in casts],
        out_specs=[
            pl.BlockSpec((1, tm, tn), rest_map),
            f32_tile(GROUP_K), f32_tile(GROUP_V),
            f32_tile(GROUP_Q), f32_tile(GROUP_K), f32_tile(GROUP_V),
        ] + [cast_spec(a) for a in casts],
        out_shape=[
            jax.ShapeDtypeStruct((N_REST, rows, E_A), F32),
            jax.ShapeDtypeStruct((rows, E_A), F32),
            jax.ShapeDtypeStruct((rows, E_A), F32),
            jax.ShapeDtypeStruct((rows, E_A), BF16),
            jax.ShapeDtypeStruct((rows, E_A), BF16),
            jax.ShapeDtypeStruct((rows, E_A), BF16),
        ] + [jax.ShapeDtypeStruct(a.shape, BF16) for a in casts],
        scratch_shapes=[pltpu.VMEM((tm, d), BF16)],
        compiler_params=_params(("arbitrary", "arbitrary")),
        name="inproj_even",
    )(x, mods, mods, w_bf, *casts)


def _attn_ctx_kernel(q_ref, k_ref, v_ref, ga_ref, o_ref):
    for h in range(N_HEADS_A):
        sl = slice(h * HEAD_DIM, (h + 1) * HEAD_DIM)
        q = q_ref[:, sl]
        k = k_ref[:, sl]
        v = v_ref[:, sl]
        s = lax.dot_general(q, k, (((1,), (1,)), ((), ())), preferred_element_type=F32) * ATTN_SCALE
        m = jnp.max(s, axis=-1, keepdims=True)
        p = jnp.exp(s - m)
        l = jnp.sum(p, axis=-1, keepdims=True)
        o = jnp.dot(p.astype(BF16), v, preferred_element_type=F32) / l
        o_ref[:, sl] = (o * _silu(ga_ref[0, :, sl])).astype(BF16)


def _attn_ctx(rest, q, k, v, seq):
    _, rows, e = rest.shape
    grp = lambda g: pl.BlockSpec((1, seq, e), lambda b, g=g: (g, b, 0))
    flat = pl.BlockSpec((seq, e), lambda b: (b, 0))
    return pl.pallas_call(
        _attn_ctx_kernel,
        grid=(rows // seq,),
        in_specs=[flat, flat, flat, grp(REST_GA)],
        out_specs=flat,
        out_shape=jax.ShapeDtypeStruct((rows, e), BF16),
        compiler_params=_params(("arbitrary",)),
        name="attn_ctx",
    )(q, k, v, rest)


def _key_window_start(chunk, n_rows):
    lo = max(chunk * Q_CHUNK_ROWS - NA_ROWS // 2, 0)
    return min(lo - lo % Q_CHUNK_ROWS, n_rows - KEY_WIN_ROWS)


def _build_bias(rpb_ref, bias_ref, head, n_rows):
    pair_w = 2 * GRID_W
    qc = lax.broadcasted_iota(jnp.int32, (GRID_W, pair_w), 0)
    lane = lax.broadcasted_iota(jnp.int32, (GRID_W, pair_w), 1)
    kc = lane % GRID_W
    rel = kc - qc + (NA_COLS - 1)
    start_c = jnp.clip(qc - NA_COLS // 2, 0, GRID_W - NA_COLS)
    valid = (kc >= start_c) & (kc < start_c + NA_COLS)
    n_dc = 2 * NA_COLS - 1
    neg = jnp.full((GRID_W, pair_w), NEG_BIG, F32)
    toeplitz = []
    for i in range(2 * NA_ROWS - 1):
        t = neg
        for dcol in range(n_dc):
            t = jnp.where(rel == dcol, rpb_ref[head, i * n_dc + dcol], t)
        toeplitz.append(jnp.where(valid, t, NEG_BIG))
    for r in range(n_rows):
        w0 = _key_window_start(r // Q_CHUNK_ROWS, n_rows)
        start_r = min(max(r - NA_ROWS // 2, 0), n_rows - NA_ROWS)

        def block(kr):
            inside = start_r <= kr < start_r + NA_ROWS
            return toeplitz[kr - r + NA_ROWS - 1] if inside else neg

        for pair in range(KEY_WIN_ROWS // 2):
            blk = jnp.where(lane < GRID_W, block(w0 + 2 * pair), block(w0 + 2 * pair + 1))
            bias_ref[r * GRID_W:(r + 1) * GRID_W, pair * pair_w:(pair + 1) * pair_w] = blk


def _attn_nbr_kernel(rpb_ref, q_ref, k_ref, v_ref, ga_ref, kc_ref, vc_ref, o_ref, bias_ref, *, n_rows):
    head = pl.program_id(0)

    @pl.when(pl.program_id(1) == 0)
    def _():
        _build_bias(rpb_ref, bias_ref, head, n_rows)

    k_ctx = kc_ref[0].astype(BF16)
    v_ctx = vc_ref[0].astype(BF16)
    qn = Q_CHUNK_ROWS * GRID_W
    kn = KEY_WIN_ROWS * GRID_W
    contract_last = (((1,), (1,)), ((), ()))
    for chunk in range(n_rows // Q_CHUNK_ROWS):
        qs = slice(chunk * qn, (chunk + 1) * qn)
        k0 = _key_window_start(chunk, n_rows) * GRID_W
        q = q_ref[qs, :]
        k_win = k_ref[k0:k0 + kn, :]
        v_win = v_ref[k0:k0 + kn, :]
        s_win = lax.dot_general(q, k_win, contract_last, preferred_element_type=F32) * ATTN_SCALE
        s_win = s_win + bias_ref[qs, :]
        s_ctx = lax.dot_general(q, k_ctx, contract_last, preferred_element_type=F32) * ATTN_SCALE
        m = jnp.maximum(jnp.max(s_win, axis=-1, keepdims=True), jnp.max(s_ctx, axis=-1, keepdims=True))
        p_win = jnp.exp(s_win - m)
        p_ctx = jnp.exp(s_ctx - m)
        l = jnp.sum(p_win, axis=-1, keepdims=True) + jnp.sum(p_ctx, axis=-1, keepdims=True)
        o = (jnp.dot(p_win.astype(BF16), v_win, preferred_element_type=F32)
             + jnp.dot(p_ctx.astype(BF16), v_ctx, preferred_element_type=F32)) / l
        o_ref[qs, :] = (o * _silu(ga_ref[0, qs, :])).astype(BF16)


def _attn_nbr(rest, q, k, v, k_ctx, v_ctx, rpb, seq, past, even_idx):
    _, rows, e = rest.shape
    n_rows = seq // GRID_W
    assert n_rows % Q_CHUNK_ROWS == 0 and n_rows >= KEY_WIN_ROWS and NA_ROWS <= n_rows
    n_bias = (2 * NA_ROWS - 1) * (2 * NA_COLS - 1)
    grp = lambda g: pl.BlockSpec((1, seq, HEAD_DIM), lambda h, b, g=g: (g, b, h))
    flat = pl.BlockSpec((seq, HEAD_DIM), lambda h, b: (b, h))
    ctx_spec = pl.BlockSpec((1, past, HEAD_DIM), lambda h, b: (b, even_idx, h))
    return pl.pallas_call(
        functools.partial(_attn_nbr_kernel, n_rows=n_rows),
        grid=(N_HEADS_A, rows // seq),
        in_specs=[
            pl.BlockSpec(memory_space=pltpu.SMEM),
            flat, flat, flat, grp(REST_GA), ctx_spec, ctx_spec,
        ],
        out_specs=flat,
        out_shape=jax.ShapeDtypeStruct((rows, e), BF16),
        scratch_shapes=[pltpu.VMEM((seq, KEY_WIN_ROWS * GRID_W), F32)],
        compiler_params=_params(("arbitrary", "arbitrary")),
        name="attn_nbr",
    )(rpb.reshape(N_HEADS_A, n_bias), q, k, v, rest, k_ctx, v_ctx)


SHORT_CHUNK = 512


def _outproj_even_kernel(oa_ref, bg_ref, cg_ref, xin_ref, gb_ref, cgp_ref, xinp_ref, cgn_ref, xinn_ref,
                         cw_ref, w_ref, x_ref, gate_ref, lng_ref, lnb_ref, o_ref, *, tm, seq):
    i = pl.program_id(0)
    row = lax.broadcasted_iota(jnp.int32, (tm, SHORT_CHUNK), 0)
    pos = (i * tm + row) % seq
    parts = [oa_ref[...]]
    for c in range(E_B // SHORT_CHUNK):
        ln = slice(c * SHORT_CHUNK, (c + 1) * SHORT_CHUNK)
        z = cg_ref[0, :, ln] * xin_ref[0, :, ln]
        z_before = cgp_ref[0, SUBLANES - 1:SUBLANES, ln] * xinp_ref[0, SUBLANES - 1:SUBLANES, ln]
        z_after = cgn_ref[0, 0:1, ln] * xinn_ref[0, 0:1, ln]
        z_prev = jnp.where(row == 0, z_before, pltpu.roll(z, 1, axis=0))
        z_prev = jnp.where(pos == 0, 0.0, z_prev)
        z_next = jnp.where(row == tm - 1, z_after, pltpu.roll(z, tm - 1, axis=0))
        z_next = jnp.where(pos == seq - 1, 0.0, z_next)
        conv = z_prev * cw_ref[0:1, ln] + z * cw_ref[1:2, ln] + z_next * cw_ref[2:3, ln]
        ob = _silu(gb_ref[0, :, ln]) * bg_ref[0, :, ln] * conv
        parts.append(ob.astype(BF16))
    lhs = jnp.concatenate(parts, axis=1)
    acc = jnp.dot(lhs, w_ref[...], preferred_element_type=F32)
    r = ALPHA * x_ref[...] + gate_ref[0] * acc
    o_ref[...] = _layernorm(r, lng_ref[...], lnb_ref[...])


def _outproj_even(oa, rest, conv_w, w_bf, x, mods, ln_g, ln_b, cond_of_tile, seq, tm):
    rows, d = x.shape
    n_sub = rows // SUBLANES
    grp = lambda g: pl.BlockSpec((1, tm, E_B), lambda i, g=g: (g, i, 0))
    before = lambda g: pl.BlockSpec(
        (1, SUBLANES, E_B), lambda i, g=g: (g, jnp.maximum(i * (tm // SUBLANES) - 1, 0), 0))
    after = lambda g: pl.BlockSpec(
        (1, SUBLANES, E_B), lambda i, g=g: (g, jnp.minimum((i + 1) * (tm // SUBLANES), n_sub - 1), 0))
    row_vec = pl.BlockSpec((1, d), lambda i: (0, 0))
    return pl.pallas_call(
        functools.partial(_outproj_even_kernel, tm=tm, seq=seq),
        grid=(rows // tm,),
        in_specs=[
            pl.BlockSpec((tm, E_A), lambda i: (i, 0)),
            grp(REST_B), grp(REST_C), grp(REST_XIN), grp(REST_GB),
            before(REST_C), before(REST_XIN), after(REST_C), after(REST_XIN),
            pl.BlockSpec((SHORT_CONV, E_B), lambda i: (0, 0)),
            _resident((E_MIX, d), lambda i: (0, 0)),
            pl.BlockSpec((tm, d), lambda i: (i, 0)),
            pl.BlockSpec((1, 1, d), lambda i: (cond_of_tile(i), 0, 2)),
            row_vec, row_vec,
        ],
        out_specs=pl.BlockSpec((tm, d), lambda i: (i, 0)),
        out_shape=jax.ShapeDtypeStruct((rows, d), F32),
        compiler_params=_params(("arbitrary",)),
        name="outproj_even",
    )(oa, rest, rest, rest, rest, rest, rest, rest, rest, conv_w, w_bf, x, mods,
      ln_g.reshape(1, d), ln_b.reshape(1, d))


CONV_GROUP = 4
ODD_PARTS = 4


def _conv_stride(seq):
    s = -(-seq // SUBLANES)
    return s if s % 2 else s + 1


def _conv_region(seq):
    rows = SUBLANES * _conv_stride(seq) + CONF_CONV
    return -(-rows // SUBLANES) * SUBLANES


def _inproj_odd_kernel(x_ref, sh_ref, sc_ref, wa_ref, wb_ref, wg_ref, cw_ref, cb_ref, sg_ref, conv_ref,
                       hb_ref, ubuf_ref, ctmp_ref, wrep_ref, *, nj, n_steps, n_slab, seq, n_seq):
    t = pl.program_id(0)
    stride = _conv_stride(seq)
    region = _conv_region(seq)
    out_rows = SUBLANES * stride

    @pl.when(t == 0)
    def _():
        ubuf_ref[...] = jnp.zeros_like(ubuf_ref)

    @pl.when((t % nj == 0) & (t < n_steps))
    def _():
        hb_ref[...] = (x_ref[...] * (1.0 + sc_ref[0]) + sh_ref[0]).astype(BF16)

    tm, d = hb_ref.shape
    part_rows = tm // ODD_PARTS

    def project(part):
        rows = slice(part * part_rows, (part + 1) * part_rows)
        h = hb_ref[rows, :]
        a = jnp.dot(h, wa_ref[...], preferred_element_type=F32)
        b = jnp.dot(h, wb_ref[...], preferred_element_type=F32)
        sg_ref[rows, :] = _silu(jnp.dot(h, wg_ref[...], preferred_element_type=F32))
        return a * _sigmoid(b)

    n_ktiles = d // MXU_DEPTH
    n_pieces = 3 * n_ktiles
    pending = []

    def paced_weights(w_ref, first_piece):
        tiles = []
        for kt in range(n_ktiles):
            w = w_ref[kt * MXU_DEPTH:(kt + 1) * MXU_DEPTH, :]
            zero = pending.pop() if pending else None
            pending.append(conv_piece(first_piece + kt))
            if zero is not None:
                w = w + jnp.tile(zero, (MXU_DEPTH // zero.shape[0], w.shape[1] // LANES))
            tiles.append(w)
        return jnp.concatenate(tiles, axis=0)

    def store_u(parts):
        u = jnp.concatenate(parts, axis=0)
        for s in range(n_seq):
            for c in range(n_slab):
                ubuf_ref[c, s * region + HALO:s * region + HALO + seq, :] = (
                    u[s * seq:(s + 1) * seq, c * LANES:(c + 1) * LANES])

    items = [(c, s, v0) for c in range(n_slab) for s in range(n_seq) for v0 in range(0, stride, CONV_GROUP)]
    per_piece = -(-len(items) // n_pieces)

    def conv_begin():
        for c in range(n_slab):
            for k in range(CONF_CONV):
                wrep_ref[c, k] = jnp.broadcast_to(cw_ref[k:k + 1, c * LANES:(c + 1) * LANES], (SUBLANES, LANES))

    def conv_piece(piece):
        bits = None
        for c, s, v0 in items[piece * per_piece:(piece + 1) * per_piece]:
            bias = jnp.broadcast_to(cb_ref[:, c * LANES:(c + 1) * LANES], (SUBLANES, LANES))
            vs = range(v0, min(v0 + CONV_GROUP, stride))
            accs = [bias for _ in vs]
            for k in range(CONF_CONV):
                wk = wrep_ref[c, k]
                for g, v in enumerate(vs):
                    first = s * region + HALO - CONF_HALF + v + k
                    accs[g] = accs[g] + ubuf_ref[c, pl.ds(first, SUBLANES, stride=stride), :] * wk
            for g, v in enumerate(vs):
                ctmp_ref[c, pl.ds(s * out_rows + v, SUBLANES, stride=stride), :] = accs[g]
                word = lax.bitcast_convert_type(accs[g], jnp.uint32)
                bits = word if bits is None else bits | word
        if bits is None:
            return None
        zero = lax.bitcast_convert_type((bits >> 16) >> 16, F32)
        return jnp.concatenate([zero, zero], axis=0).astype(BF16)

    def conv_end():
        for c in range(n_slab):
            for s in range(n_seq):
                conv_ref[c, s * seq:(s + 1) * seq, :] = ctmp_ref[c, s * out_rows:s * out_rows + seq, :]

    @pl.when(t == 0)
    def _():
        store_u([project(part) for part in range(ODD_PARTS)])

    @pl.when((t > 0) & (t < n_steps))
    def _():
        conv_begin()
        h = hb_ref[...]
        a = jnp.dot(h, paced_weights(wa_ref, 0), preferred_element_type=F32)
        b = jnp.dot(h, paced_weights(wb_ref, n_ktiles), preferred_element_type=F32)
        sg_ref[...] = _silu(jnp.dot(h, paced_weights(wg_ref, 2 * n_ktiles), preferred_element_type=F32))
        conv_end()
        store_u([a * _sigmoid(b)])

    @pl.when(t == n_steps)
    def _():
        conv_begin()
        for piece in range(n_pieces):
            conv_piece(piece)
        conv_end()


def _inproj_odd(x, mods, w_bf, conv_w, conv_b, cond_of_tile, seq, tm, tn):
    rows, d = x.shape
    nj = E_C // tn
    n_steps = (rows // tm) * nj
    n_slab = tn // LANES
    n_seq = tm // seq
    assert tm % seq == 0
    cur = lambda t: jnp.minimum(t, n_steps - 1)
    prev = lambda t: jnp.maximum(t - 1, 0)
    wspec = lambda part: pl.BlockSpec((d, tn), lambda t, part=part: (0, part * nj + cur(t) % nj))
    return pl.pallas_call(
        functools.partial(_inproj_odd_kernel, nj=nj, n_steps=n_steps, n_slab=n_slab, seq=seq, n_seq=n_seq),
        grid=(n_steps + 1,),
        in_specs=[
            pl.BlockSpec((tm, d), lambda t: (cur(t) // nj, 0)),
            pl.BlockSpec((1, 1, d), lambda t: (cond_of_tile(cur(t) // nj), 0, 0)),
            pl.BlockSpec((1, 1, d), lambda t: (cond_of_tile(cur(t) // nj), 0, 1)),
            wspec(0), wspec(1), wspec(2),
            pl.BlockSpec((CONF_CONV, tn), lambda t: (0, prev(t) % nj)),
            pl.BlockSpec((1, tn), lambda t: (0, prev(t) % nj)),
        ],
        out_specs=[
            pl.BlockSpec((tm, tn), lambda t: (cur(t) // nj, cur(t) % nj)),
            pl.BlockSpec((n_slab, tm, LANES), lambda t: (prev(t) % nj, prev(t) // nj, 0)),
        ],
        out_shape=[
            jax.ShapeDtypeStruct((rows, E_C), F32),
            jax.ShapeDtypeStruct((E_C // LANES, rows, LANES), F32),
        ],
        scratch_shapes=[
            pltpu.VMEM((tm, d), BF16),
            pltpu.VMEM((n_slab, n_seq * _conv_region(seq), LANES), F32),
            pltpu.VMEM((n_slab, n_seq * SUBLANES * _conv_stride(seq), LANES), F32),
            pltpu.VMEM((n_slab, CONF_CONV, SUBLANES, LANES), F32),
        ],
        compiler_params=_params(("arbitrary",)),
        name="inproj_odd",
    )(x, mods, mods, w_bf, w_bf, w_bf, conv_w, conv_b.reshape(1, E_C))


def _outproj_odd_kernel(conv_ref, sg_ref, cng_ref, cnb_ref, w_ref, x_ref, gate_ref, lng_ref, lnb_ref, o_ref,
                        *, n_slab):
    total = conv_ref[0]
    for c in range(1, n_slab):
        total = total + conv_ref[c]
    mu = jnp.sum(total, axis=-1, keepdims=True) * (1.0 / E_C)
    sq = None
    for c in range(n_slab):
        dlt = conv_ref[c] - mu
        sq = dlt * dlt if sq is None else sq + dlt * dlt
    rstd = lax.rsqrt(jnp.sum(sq, axis=-1, keepdims=True) * (1.0 / E_C) + LN_EPS)
    parts = []
    for c in range(n_slab):
        ln = slice(c * LANES, (c + 1) * LANES)
        v = (conv_ref[c] - mu) * rstd * cng_ref[:, ln] + cnb_ref[:, ln]
        parts.append((_silu(v) * sg_ref[:, ln]).astype(BF16))
    lhs = jnp.concatenate(parts, axis=1)
    acc = jnp.dot(lhs, w_ref[...], preferred_element_type=F32)
    r = ALPHA * x_ref[...] + gate_ref[0] * acc
    o_ref[...] = _layernorm(r, lng_ref[...], lnb_ref[...])


def _outproj_odd(conv, sg, cn_g, cn_b, w_bf, x, mods, ln_g, ln_b, cond_of_tile, tm):
    rows, d = x.shape
    n_slab = E_C // LANES
    row_vec = pl.BlockSpec((1, d), lambda i: (0, 0))
    chan_vec = pl.BlockSpec((1, E_C), lambda i: (0, 0))
    return pl.pallas_call(
        functools.partial(_outproj_odd_kernel, n_slab=n_slab),
        grid=(rows // tm,),
        in_specs=[
            pl.BlockSpec((n_slab, tm, LANES), lambda i: (0, i, 0)),
            pl.BlockSpec((tm, E_C), lambda i: (i, 0)),
            chan_vec, chan_vec,
            _resident((E_C, d), lambda i: (0, 0)),
            pl.BlockSpec((tm, d), lambda i: (i, 0)),
            pl.BlockSpec((1, 1, d), lambda i: (cond_of_tile(i), 0, 2)),
            row_vec, row_vec,
        ],
        out_specs=pl.BlockSpec((tm, d), lambda i: (i, 0)),
        out_shape=jax.ShapeDtypeStruct((rows, d), F32),
        compiler_params=_params(("arbitrary",)),
        name="outproj_odd",
    )(conv, sg, cn_g.reshape(1, E_C), cn_b.reshape(1, E_C), w_bf, x, mods,
      ln_g.reshape(1, d), ln_b.reshape(1, d))


IN_ROWS = 1024
OUT_ROWS = 256
EVEN_COLS = 512
ODD_COLS = 256


def kernel(x_prompt, x_sample, cache_k, cache_v, c, c_ctx, w_ada, b_ada, ln_g, ln_b, w_in_even, conv_short_w,
           rpb, w_out_even, w_in_odd, conv_conf_w, conv_conf_b, ln_conf_g, ln_conf_b, w_out_odd):
    batch, seq, d = x_prompt.shape
    dec_batch, dec_seq, _ = x_sample.shape
    n_even, past = cache_k.shape[1], cache_k.shape[2]
    assert 1 + dec_batch <= N_COND and d == D_MODEL

    cond = jnp.zeros((N_COND, d), F32).at[0].set(c_ctx).at[1:1 + dec_batch].set(c)
    mods_all = _ada(cond, w_ada, b_ada)
    k_ctx = cache_k.reshape(dec_batch, n_even * past, E_A)
    v_ctx = cache_v.reshape(dec_batch, n_even * past, E_A)

    def cond_fn(is_prompt, tm):
        return (lambda i: 0) if is_prompt else (lambda i: 1 + (i * tm) // dec_seq)

    y_p = x_prompt.reshape(batch * seq, d)
    y_s = x_sample.reshape(dec_batch * dec_seq, d)
    streams = lambda: ((y_p, seq, True), (y_s, dec_seq, False))
    bf16_weights = {}

    def as_bf16(name, w):
        return bf16_weights[name] if name in bf16_weights else w.astype(BF16)

    new_k, new_v = [], []
    for layer in range(DEPTH):
        mods = mods_all[layer].reshape(N_COND, 1, 3 * d)
        half = layer // 2
        has_next = layer + 1 < DEPTH
        outs = []
        if layer % 2 == 0:
            side = {True: [("in_odd", w_in_odd)] if has_next else [],
                    False: [("out_even", w_out_even)] + ([("out_odd", w_out_odd)] if has_next else [])}
            w_in = w_in_even[half].astype(BF16)
            proj = {}
            for y, sq, is_prompt in streams():
                tm_in = min(y.shape[0], IN_ROWS)
                res = _inproj_even(y, mods, w_in, cond_fn(is_prompt, tm_in), tm_in, EVEN_COLS,
                                   [w[half] for _, w in side[is_prompt]])
                proj[is_prompt] = res[:6]
                for (name, _), w16 in zip(side[is_prompt], res[6:]):
                    bf16_weights[(name, half)] = w16
            w_out = as_bf16(("out_even", half), w_out_even[half])
            for y, sq, is_prompt in streams():
                rest, k, v, q16, k16, v16 = proj[is_prompt]
                tm_out = min(y.shape[0], OUT_ROWS)
                if is_prompt:
                    oa = _attn_ctx(rest, q16, k16, v16, sq)
                    new_k.append(k.reshape(batch, seq, N_HEADS_A, HEAD_DIM))
                    new_v.append(v.reshape(batch, seq, N_HEADS_A, HEAD_DIM))
                else:
                    oa = _attn_nbr(rest, q16, k16, v16, k_ctx, v_ctx, rpb[half], sq, past, half)
                outs.append(_outproj_even(oa, rest, conv_short_w[half], w_out, y, mods,
                                          ln_g[layer], ln_b[layer], cond_fn(is_prompt, tm_out), sq, tm_out))
        else:
            w_in = as_bf16(("in_odd", half), w_in_odd[half])
            w_out = as_bf16(("out_odd", half), w_out_odd[half])
            for y, sq, is_prompt in streams():
                tm_in = min(y.shape[0], IN_ROWS)
                tm_out = min(y.shape[0], OUT_ROWS)
                sg, conv = _inproj_odd(y, mods, w_in, conv_conf_w[half], conv_conf_b[half],
                                       cond_fn(is_prompt, tm_in), sq, tm_in, ODD_COLS)
                outs.append(_outproj_odd(conv, sg, ln_conf_g[half], ln_conf_b[half], w_out, y, mods,
                                         ln_g[layer], ln_b[layer], cond_fn(is_prompt, tm_out), tm_out))
        y_p, y_s = outs
    return (y_p.reshape(batch, seq, d), y_s.reshape(dec_batch, dec_seq, d),
            jnp.stack(new_k, axis=1), jnp.stack(new_v, axis=1))
```

```python
import functools

import jax
import jax.numpy as jnp
from jax import lax
from jax.experimental import pallas as pl
from jax.experimental.pallas import tpu as pltpu

F32 = jnp.float32
BF16 = jnp.bfloat16

D_MODEL = 2048
DEPTH = 2
GRID_W = 64
E_MIX = 2 * D_MODEL
E_A = E_MIX // 2
E_B = E_MIX - E_A
HEAD_DIM = 128
N_HEADS_A = E_A // HEAD_DIM
NA_ROWS = 8
NA_COLS = 16
SHORT_CONV = 3
E_C = E_MIX
CONF_CONV = 31
CONF_HALF = CONF_CONV // 2
ALPHA = (2.0 * DEPTH) ** 0.25
LN_EPS = 1e-5
NEG_BIG = -1e30
ATTN_SCALE = HEAD_DIM ** -0.5

N_COND = 8
SUBLANES = 8
LANES = 128
MXU_DEPTH = 256
HALO = 16
VMEM_LIMIT = 56 * 1024 * 1024

Q_CHUNK_ROWS = 4
KEY_WIN_ROWS = 12


def _params(semantics):
    return pltpu.CompilerParams(dimension_semantics=semantics, vmem_limit_bytes=VMEM_LIMIT)


def _resident(block_shape, index_map):
    return pl.BlockSpec(block_shape, index_map, pipeline_mode=pl.Buffered(1))


def _sigmoid(x):
    return 0.5 * jnp.tanh(0.5 * x) + 0.5


def _silu(x):
    h = 0.5 * x
    return h + h * jnp.tanh(h)


def _layernorm(r, g, b):
    mu = jnp.mean(r, axis=-1, keepdims=True)
    d = r - mu
    var = jnp.mean(d * d, axis=-1, keepdims=True)
    return d * lax.rsqrt(var + LN_EPS) * g + b


def _ada_kernel(cond_ref, w_ref, b_ref, o_ref):
    s = _silu(cond_ref[...]).astype(BF16)
    o_ref[0] = jnp.dot(s, w_ref[0].astype(BF16), preferred_element_type=F32) + b_ref[0]


def _ada(cond, w_ada, b_ada):
    depth, d, n3 = w_ada.shape
    tn = 1024
    return pl.pallas_call(
        _ada_kernel,
        grid=(depth, n3 // tn),
        in_specs=[
            pl.BlockSpec((N_COND, d), lambda l, j: (0, 0)),
            pl.BlockSpec((1, d, tn), lambda l, j: (l, 0, j)),
            pl.BlockSpec((1, 1, tn), lambda l, j: (l, 0, j)),
        ],
        out_specs=pl.BlockSpec((1, N_COND, tn), lambda l, j: (l, 0, j)),
        out_shape=jax.ShapeDtypeStruct((depth, N_COND, n3), F32),
        compiler_params=_params(("arbitrary", "arbitrary")),
        name="ada",
    )(cond, w_ada, b_ada.reshape(depth, 1, n3))


GROUP_Q, GROUP_K, GROUP_V = 0, 1, 2
N_REST = 5
REST_GA, REST_B, REST_C, REST_XIN, REST_GB = range(N_REST)
BF16_ROWS = 16


def _inproj_even_kernel(*refs, npg, n_casts, n_merge):
    n_in = 4 + n_casts + n_merge
    x_ref, sh_ref, sc_ref, w_ref = refs[:4]
    cast_src = refs[4:4 + n_casts]
    merge_src = refs[4 + n_casts:n_in]
    rest_ref, k_ref, v_ref, q16_ref, k16_ref, v16_ref = refs[n_in:n_in + 6]
    cast_dst = refs[n_in + 6:n_in + 6 + n_casts]
    merge_dst = refs[n_in + 6 + n_casts:-1]
    hb_ref = refs[-1]
    j = pl.program_id(1)

    @pl.when(j == 0)
    def _():
        hb_ref[...] = (x_ref[...] * (1.0 + sc_ref[0]) + sh_ref[0]).astype(BF16)

    for src, dst in zip(cast_src, cast_dst):
        dst[...] = src[...].astype(BF16)

    for src, dst in zip(merge_src, merge_dst):
        for h in range(N_HEADS_A):
            dst[:, h * HEAD_DIM:(h + 1) * HEAD_DIM] = src[pl.ds(h, dst.shape[0], stride=N_HEADS_A), :]

    group = j // npg

    @pl.when(group == GROUP_Q)
    def _():
        q16_ref[...] = jnp.dot(hb_ref[...], w_ref[...], preferred_element_type=F32).astype(BF16)

    @pl.when(group == GROUP_K)
    def _():
        r = jnp.dot(hb_ref[...], w_ref[...], preferred_element_type=F32)
        k_ref[...] = r
        k16_ref[...] = r.astype(BF16)

    @pl.when(group == GROUP_V)
    def _():
        r = jnp.dot(hb_ref[...], w_ref[...], preferred_element_type=F32)
        v_ref[...] = r
        v16_ref[...] = r.astype(BF16)

    @pl.when(group > GROUP_V)
    def _():
        rest_ref[0] = jnp.dot(hb_ref[...], w_ref[...], preferred_element_type=F32)


def _inproj_even(x, mods, w_bf, cond_of_tile, tm, tn, casts, caches=(), even_idx=0):
    rows, d = x.shape
    n = w_bf.shape[1]
    npg = E_A // tn
    nj = n // tn
    n_steps = (rows // tm) * nj

    def rest_map(i, j):
        group = j // npg
        return jnp.maximum(group - 3, 0), i, jnp.where(group >= 3, j % npg, 0)

    col_map = lambda g: (lambda i, j: (i, jnp.clip(j - g * npg, 0, npg - 1)))

    def cast_spec(a):
        r, c = a.shape
        rb = max(BF16_ROWS, r // n_steps)
        assert r % rb == 0 and n_steps % (r // rb) == 0
        rep = n_steps // (r // rb)
        return pl.BlockSpec((rb, c), lambda i, j: ((i * nj + j) // rep, 0))

    merge_in, merge_out, merge_shapes, flat_caches = [], [], [], []
    for cache in caches:
        cb, n_even, past, heads, dim = cache.shape
        rb = (cb * past) // n_steps
        assert heads * dim == E_A and rb * n_steps == cb * past and past % rb == 0 and rb % SUBLANES == 0
        per_batch = past // rb

        def src_map(i, j, n_even=n_even, per_batch=per_batch):
            step = i * nj + j
            return (step // per_batch * n_even + even_idx) * per_batch + step % per_batch, 0

        merge_in.append(pl.BlockSpec((rb * heads, dim), src_map))
        merge_out.append(pl.BlockSpec((rb, E_A), lambda i, j: (i * nj + j, 0)))
        merge_shapes.append(jax.ShapeDtypeStruct((cb * past, E_A), cache.dtype))
        flat_caches.append(cache.reshape(cb * n_even * past * heads, dim))

    tile = lambda g: pl.BlockSpec((tm, tn), col_map(g))
    return pl.pallas_call(
        functools.partial(_inproj_even_kernel, npg=npg, n_casts=len(casts), n_merge=len(caches)),
        grid=(rows // tm, nj),
        in_specs=[
            pl.BlockSpec((tm, d), lambda i, j: (i, 0)),
            pl.BlockSpec((1, 1, d), lambda i, j: (cond_of_tile(i), 0, 0)),
            pl.BlockSpec((1, 1, d), lambda i, j: (cond_of_tile(i), 0, 1)),
            pl.BlockSpec((d, tn), lambda i, j: (0, j)),
        ] + [cast_spec(a) for a in casts] + merge_in,
        out_specs=[
            pl.BlockSpec((1, tm, tn), rest_map),
            tile(GROUP_K), tile(GROUP_V),
            tile(GROUP_Q), tile(GROUP_K), tile(GROUP_V),
        ] + [cast_spec(a) for a in casts] + merge_out,
        out_shape=[
            jax.ShapeDtypeStruct((N_REST, rows, E_A), F32),
            jax.ShapeDtypeStruct((rows, E_A), F32),
            jax.ShapeDtypeStruct((rows, E_A), F32),
            jax.ShapeDtypeStruct((rows, E_A), BF16),
            jax.ShapeDtypeStruct((rows, E_A), BF16),
            jax.ShapeDtypeStruct((rows, E_A), BF16),
        ] + [jax.ShapeDtypeStruct(a.shape, BF16) for a in casts] + merge_shapes,
        scratch_shapes=[pltpu.VMEM((tm, d), BF16)],
        compiler_params=_params(("arbitrary", "arbitrary")),
        name="inproj_even",
    )(x, mods, mods, w_bf, *casts, *flat_caches)


def _attn_ctx_kernel(q_ref, k_ref, v_ref, ga_ref, o_ref):
    for h in range(N_HEADS_A):
        sl = slice(h * HEAD_DIM, (h + 1) * HEAD_DIM)
        q = q_ref[:, sl]
        k = k_ref[:, sl]
        v = v_ref[:, sl]
        s = lax.dot_general(q, k, (((1,), (1,)), ((), ())), preferred_element_type=F32) * ATTN_SCALE
        m = jnp.max(s, axis=-1, keepdims=True)
        p = jnp.exp(s - m)
        l = jnp.sum(p, axis=-1, keepdims=True)
        o = jnp.dot(p.astype(BF16), v, preferred_element_type=F32) / l
        o_ref[:, sl] = (o * _silu(ga_ref[0, :, sl])).astype(BF16)


def _attn_ctx(rest, q, k, v, seq):
    _, rows, e = rest.shape
    grp = lambda g: pl.BlockSpec((1, seq, e), lambda b, g=g: (g, b, 0))
    flat = pl.BlockSpec((seq, e), lambda b: (b, 0))
    return pl.pallas_call(
        _attn_ctx_kernel,
        grid=(rows // seq,),
        in_specs=[flat, flat, flat, grp(REST_GA)],
        out_specs=flat,
        out_shape=jax.ShapeDtypeStruct((rows, e), BF16),
        compiler_params=_params(("arbitrary",)),
        name="attn_ctx",
    )(q, k, v, rest)


def _key_window_start(chunk, n_rows):
    lo = max(chunk * Q_CHUNK_ROWS - NA_ROWS // 2, 0)
    return min(lo - lo % Q_CHUNK_ROWS, n_rows - KEY_WIN_ROWS)


def _build_bias(rpb_ref, bias_ref, head, n_rows):
    pair_w = 2 * GRID_W
    qc = lax.broadcasted_iota(jnp.int32, (GRID_W, pair_w), 0)
    lane = lax.broadcasted_iota(jnp.int32, (GRID_W, pair_w), 1)
    kc = lane % GRID_W
    rel = kc - qc + (NA_COLS - 1)
    start_c = jnp.clip(qc - NA_COLS // 2, 0, GRID_W - NA_COLS)
    valid = (kc >= start_c) & (kc < start_c + NA_COLS)
    n_dc = 2 * NA_COLS - 1
    neg = jnp.full((GRID_W, pair_w), NEG_BIG, F32)
    toeplitz = []
    for i in range(2 * NA_ROWS - 1):
        t = neg
        for dcol in range(n_dc):
            t = jnp.where(rel == dcol, rpb_ref[head, i * n_dc + dcol], t)
        toeplitz.append(jnp.where(valid, t, NEG_BIG))
    for r in range(n_rows):
        w0 = _key_window_start(r // Q_CHUNK_ROWS, n_rows)
        start_r = min(max(r - NA_ROWS // 2, 0), n_rows - NA_ROWS)

        def block(kr):
            inside = start_r <= kr < start_r + NA_ROWS
            return toeplitz[kr - r + NA_ROWS - 1] if inside else neg

        for pair in range(KEY_WIN_ROWS // 2):
            blk = jnp.where(lane < GRID_W, block(w0 + 2 * pair), block(w0 + 2 * pair + 1))
            bias_ref[r * GRID_W:(r + 1) * GRID_W, pair * pair_w:(pair + 1) * pair_w] = blk


def _attn_nbr_kernel(rpb_ref, q_ref, k_ref, v_ref, ga_ref, kc_ref, vc_ref, o_ref, bias_ref, *, n_rows):
    head = pl.program_id(0)

    @pl.when(pl.program_id(1) == 0)
    def _():
        _build_bias(rpb_ref, bias_ref, head, n_rows)

    k_ctx = kc_ref[0].astype(BF16)
    v_ctx = vc_ref[0].astype(BF16)
    qn = Q_CHUNK_ROWS * GRID_W
    kn = KEY_WIN_ROWS * GRID_W
    contract_last = (((1,), (1,)), ((), ()))
    for chunk in range(n_rows // Q_CHUNK_ROWS):
        qs = slice(chunk * qn, (chunk + 1) * qn)
        k0 = _key_window_start(chunk, n_rows) * GRID_W
        q = q_ref[qs, :]
        k_win = k_ref[k0:k0 + kn, :]
        v_win = v_ref[k0:k0 + kn, :]
        s_win = lax.dot_general(q, k_win, contract_last, preferred_element_type=F32) * ATTN_SCALE
        s_win = s_win + bias_ref[qs, :]
        s_ctx = lax.dot_general(q, k_ctx, contract_last, preferred_element_type=F32) * ATTN_SCALE
        m = jnp.maximum(jnp.max(s_win, axis=-1, keepdims=True), jnp.max(s_ctx, axis=-1, keepdims=True))
        p_win = jnp.exp(s_win - m)
        p_ctx = jnp.exp(s_ctx - m)
        l = jnp.sum(p_win, axis=-1, keepdims=True) + jnp.sum(p_ctx, axis=-1, keepdims=True)
        o = (jnp.dot(p_win.astype(BF16), v_win, preferred_element_type=F32)
             + jnp.dot(p_ctx.astype(BF16), v_ctx, preferred_element_type=F32)) / l
        o_ref[qs, :] = (o * _silu(ga_ref[0, qs, :])).astype(BF16)


def _attn_nbr(rest, q, k, v, k_ctx, v_ctx, rpb, seq, past, even_idx):
    _, rows, e = rest.shape
    n_rows = seq // GRID_W
    assert n_rows % Q_CHUNK_ROWS == 0 and n_rows >= KEY_WIN_ROWS and NA_ROWS <= n_rows
    n_bias = (2 * NA_ROWS - 1) * (2 * NA_COLS - 1)
    grp = lambda g: pl.BlockSpec((1, seq, HEAD_DIM), lambda h, b, g=g: (g, b, h))
    flat = pl.BlockSpec((seq, HEAD_DIM), lambda h, b: (b, h))
    ctx_spec = pl.BlockSpec((1, past, HEAD_DIM), lambda h, b: (b, even_idx, h))
    return pl.pallas_call(
        functools.partial(_attn_nbr_kernel, n_rows=n_rows),
        grid=(N_HEADS_A, rows // seq),
        in_specs=[
            pl.BlockSpec(memory_space=pltpu.SMEM),
            flat, flat, flat, grp(REST_GA), ctx_spec, ctx_spec,
        ],
        out_specs=flat,
        out_shape=jax.ShapeDtypeStruct((rows, e), BF16),
        scratch_shapes=[pltpu.VMEM((seq, KEY_WIN_ROWS * GRID_W), F32)],
        compiler_params=_params(("arbitrary", "arbitrary")),
        name="attn_nbr",
    )(rpb.reshape(N_HEADS_A, n_bias), q, k, v, rest, k_ctx, v_ctx)


SHORT_CHUNK = 512


def _outproj_even_kernel(oa_ref, bg_ref, cg_ref, xin_ref, gb_ref, cgp_ref, xinp_ref, cgn_ref, xinn_ref,
                         cw_ref, w_ref, x_ref, gate_ref, lng_ref, lnb_ref, o_ref, *, tm, seq):
    i = pl.program_id(0)
    row = lax.broadcasted_iota(jnp.int32, (tm, SHORT_CHUNK), 0)
    pos = (i * tm + row) % seq
    parts = [oa_ref[...]]
    for c in range(E_B // SHORT_CHUNK):
        ln = slice(c * SHORT_CHUNK, (c + 1) * SHORT_CHUNK)
        z = cg_ref[0, :, ln] * xin_ref[0, :, ln]
        z_before = cgp_ref[0, SUBLANES - 1:SUBLANES, ln] * xinp_ref[0, SUBLANES - 1:SUBLANES, ln]
        z_after = cgn_ref[0, 0:1, ln] * xinn_ref[0, 0:1, ln]
        z_prev = jnp.where(row == 0, z_before, pltpu.roll(z, 1, axis=0))
        z_prev = jnp.where(pos == 0, 0.0, z_prev)
        z_next = jnp.where(row == tm - 1, z_after, pltpu.roll(z, tm - 1, axis=0))
        z_next = jnp.where(pos == seq - 1, 0.0, z_next)
        conv = z_prev * cw_ref[0:1, ln] + z * cw_ref[1:2, ln] + z_next * cw_ref[2:3, ln]
        ob = _silu(gb_ref[0, :, ln]) * bg_ref[0, :, ln] * conv
        parts.append(ob.astype(BF16))
    lhs = jnp.concatenate(parts, axis=1)
    acc = jnp.dot(lhs, w_ref[...], preferred_element_type=F32)
    r = ALPHA * x_ref[...] + gate_ref[0] * acc
    o_ref[...] = _layernorm(r, lng_ref[...], lnb_ref[...])


def _outproj_even(oa, rest, conv_w, w_bf, x, mods, ln_g, ln_b, cond_of_tile, seq, tm):
    rows, d = x.shape
    n_sub = rows // SUBLANES
    grp = lambda g: pl.BlockSpec((1, tm, E_B), lambda i, g=g: (g, i, 0))
    before = lambda g: pl.BlockSpec(
        (1, SUBLANES, E_B), lambda i, g=g: (g, jnp.maximum(i * (tm // SUBLANES) - 1, 0), 0))
    after = lambda g: pl.BlockSpec(
        (1, SUBLANES, E_B), lambda i, g=g: (g, jnp.minimum((i + 1) * (tm // SUBLANES), n_sub - 1), 0))
    row_vec = pl.BlockSpec((1, d), lambda i: (0, 0))
    return pl.pallas_call(
        functools.partial(_outproj_even_kernel, tm=tm, seq=seq),
        grid=(rows // tm,),
        in_specs=[
            pl.BlockSpec((tm, E_A), lambda i: (i, 0)),
            grp(REST_B), grp(REST_C), grp(REST_XIN), grp(REST_GB),
            before(REST_C), before(REST_XIN), after(REST_C), after(REST_XIN),
            pl.BlockSpec((SHORT_CONV, E_B), lambda i: (0, 0)),
            _resident((E_MIX, d), lambda i: (0, 0)),
            pl.BlockSpec((tm, d), lambda i: (i, 0)),
            pl.BlockSpec((1, 1, d), lambda i: (cond_of_tile(i), 0, 2)),
            row_vec, row_vec,
        ],
        out_specs=pl.BlockSpec((tm, d), lambda i: (i, 0)),
        out_shape=jax.ShapeDtypeStruct((rows, d), F32),
        compiler_params=_params(("arbitrary",)),
        name="outproj_even",
    )(oa, rest, rest, rest, rest, rest, rest, rest, rest, conv_w, w_bf, x, mods,
      ln_g.reshape(1, d), ln_b.reshape(1, d))


CONV_GROUP = 4
ODD_PARTS = 4


def _conv_stride(seq):
    s = -(-seq // SUBLANES)
    return s if s % 2 else s + 1


def _conv_region(seq):
    rows = SUBLANES * _conv_stride(seq) + CONF_CONV
    return -(-rows // SUBLANES) * SUBLANES


def _inproj_odd_kernel(x_ref, sh_ref, sc_ref, wa_ref, wb_ref, wg_ref, cw_ref, cb_ref, sg_ref, conv_ref,
                       hb_ref, ubuf_ref, ctmp_ref, wrep_ref, *, nj, n_steps, n_slab, seq, n_seq):
    t = pl.program_id(0)
    stride = _conv_stride(seq)
    region = _conv_region(seq)
    out_rows = SUBLANES * stride

    @pl.when(t == 0)
    def _():
        ubuf_ref[...] = jnp.zeros_like(ubuf_ref)

    @pl.when((t % nj == 0) & (t < n_steps))
    def _():
        hb_ref[...] = (x_ref[...] * (1.0 + sc_ref[0]) + sh_ref[0]).astype(BF16)

    tm, d = hb_ref.shape
    part_rows = tm // ODD_PARTS

    def project(part):
        rows = slice(part * part_rows, (part + 1) * part_rows)
        h = hb_ref[rows, :]
        a = jnp.dot(h, wa_ref[...], preferred_element_type=F32)
        b = jnp.dot(h, wb_ref[...], preferred_element_type=F32)
        sg_ref[rows, :] = _silu(jnp.dot(h, wg_ref[...], preferred_element_type=F32))
        return a * _sigmoid(b)

    n_ktiles = d // MXU_DEPTH
    n_pieces = 3 * n_ktiles
    pending = []

    def paced_weights(w_ref, first_piece):
        tiles = []
        for kt in range(n_ktiles):
            w = w_ref[kt * MXU_DEPTH:(kt + 1) * MXU_DEPTH, :]
            zero = pending.pop() if pending else None
            pending.append(conv_piece(first_piece + kt))
            if zero is not None:
                w = w + jnp.tile(zero, (MXU_DEPTH // zero.shape[0], w.shape[1] // LANES))
            tiles.append(w)
        return jnp.concatenate(tiles, axis=0)

    def store_u(parts):
        u = jnp.concatenate(parts, axis=0)
        for s in range(n_seq):
            for c in range(n_slab):
                ubuf_ref[c, s * region + HALO:s * region + HALO + seq, :] = (
                    u[s * seq:(s + 1) * seq, c * LANES:(c + 1) * LANES])

    items = [(c, s, v0) for c in range(n_slab) for s in range(n_seq) for v0 in range(0, stride, CONV_GROUP)]
    per_piece = -(-len(items) // n_pieces)

    def conv_begin():
        for c in range(n_slab):
            for k in range(CONF_CONV):
                wrep_ref[c, k] = jnp.broadcast_to(cw_ref[k:k + 1, c * LANES:(c + 1) * LANES], (SUBLANES, LANES))

    def conv_piece(piece):
        bits = None
        for c, s, v0 in items[piece * per_piece:(piece + 1) * per_piece]:
            bias = jnp.broadcast_to(cb_ref[:, c * LANES:(c + 1) * LANES], (SUBLANES, LANES))
            vs = range(v0, min(v0 + CONV_GROUP, stride))
            accs = [bias for _ in vs]
            for k in range(CONF_CONV):
                wk = wrep_ref[c, k]
                for g, v in enumerate(vs):
                    first = s * region + HALO - CONF_HALF + v + k
                    accs[g] = accs[g] + ubuf_ref[c, pl.ds(first, SUBLANES, stride=stride), :] * wk
            for g, v in enumerate(vs):
                ctmp_ref[c, pl.ds(s * out_rows + v, SUBLANES, stride=stride), :] = accs[g]
                word = lax.bitcast_convert_type(accs[g], jnp.uint32)
                bits = word if bits is None else bits | word
        if bits is None:
            return None
        zero = lax.bitcast_convert_type((bits >> 16) >> 16, F32)
        return jnp.concatenate([zero, zero], axis=0).astype(BF16)

    def conv_end():
        for c in range(n_slab):
            for s in range(n_seq):
                conv_ref[c, s * seq:(s + 1) * seq, :] = ctmp_ref[c, s * out_rows:s * out_rows + seq, :]

    @pl.when(t == 0)
    def _():
        store_u([project(part) for part in range(ODD_PARTS)])

    @pl.when((t > 0) & (t < n_steps))
    def _():
        conv_begin()
        h = hb_ref[...]
        a = jnp.dot(h, paced_weights(wa_ref, 0), preferred_element_type=F32)
        b = jnp.dot(h, paced_weights(wb_ref, n_ktiles), preferred_element_type=F32)
        sg_ref[...] = _silu(jnp.dot(h, paced_weights(wg_ref, 2 * n_ktiles), preferred_element_type=F32))
        conv_end()
        store_u([a * _sigmoid(b)])

    @pl.when(t == n_steps)
    def _():
        conv_begin()
        for piece in range(n_pieces):
            conv_piece(piece)
        conv_end()


def _inproj_odd(x, mods, w_bf, conv_w, conv_b, cond_of_tile, seq, tm, tn):
    rows, d = x.shape
    nj = E_C // tn
    n_steps = (rows // tm) * nj
    n_slab = tn // LANES
    n_seq = tm // seq
    assert tm % seq == 0
    cur = lambda t: jnp.minimum(t, n_steps - 1)
    prev = lambda t: jnp.maximum(t - 1, 0)
    wspec = lambda part: pl.BlockSpec((d, tn), lambda t, part=part: (0, part * nj + cur(t) % nj))
    return pl.pallas_call(
        functools.partial(_inproj_odd_kernel, nj=nj, n_steps=n_steps, n_slab=n_slab, seq=seq, n_seq=n_seq),
        grid=(n_steps + 1,),
        in_specs=[
            pl.BlockSpec((tm, d), lambda t: (cur(t) // nj, 0)),
            pl.BlockSpec((1, 1, d), lambda t: (cond_of_tile(cur(t) // nj), 0, 0)),
            pl.BlockSpec((1, 1, d), lambda t: (cond_of_tile(cur(t) // nj), 0, 1)),
            wspec(0), wspec(1), wspec(2),
            pl.BlockSpec((CONF_CONV, tn), lambda t: (0, prev(t) % nj)),
            pl.BlockSpec((1, tn), lambda t: (0, prev(t) % nj)),
        ],
        out_specs=[
            pl.BlockSpec((tm, tn), lambda t: (cur(t) // nj, cur(t) % nj)),
            pl.BlockSpec((n_slab, tm, LANES), lambda t: (prev(t) % nj, prev(t) // nj, 0)),
        ],
        out_shape=[
            jax.ShapeDtypeStruct((rows, E_C), F32),
            jax.ShapeDtypeStruct((E_C // LANES, rows, LANES), F32),
        ],
        scratch_shapes=[
            pltpu.VMEM((tm, d), BF16),
            pltpu.VMEM((n_slab, n_seq * _conv_region(seq), LANES), F32),
            pltpu.VMEM((n_slab, n_seq * SUBLANES * _conv_stride(seq), LANES), F32),
            pltpu.VMEM((n_slab, CONF_CONV, SUBLANES, LANES), F32),
        ],
        compiler_params=_params(("arbitrary",)),
        name="inproj_odd",
    )(x, mods, mods, w_bf, w_bf, w_bf, conv_w, conv_b.reshape(1, E_C))


def _outproj_odd_kernel(conv_ref, sg_ref, cng_ref, cnb_ref, w_ref, x_ref, gate_ref, lng_ref, lnb_ref, o_ref,
                        *, n_slab):
    total = conv_ref[0]
    for c in range(1, n_slab):
        total = total + conv_ref[c]
    mu = jnp.sum(total, axis=-1, keepdims=True) * (1.0 / E_C)
    sq = None
    for c in range(n_slab):
        dlt = conv_ref[c] - mu
        sq = dlt * dlt if sq is None else sq + dlt * dlt
    rstd = lax.rsqrt(jnp.sum(sq, axis=-1, keepdims=True) * (1.0 / E_C) + LN_EPS)
    parts = []
    for c in range(n_slab):
        ln = slice(c * LANES, (c + 1) * LANES)
        v = (conv_ref[c] - mu) * rstd * cng_ref[:, ln] + cnb_ref[:, ln]
        parts.append((_silu(v) * sg_ref[:, ln]).astype(BF16))
    lhs = jnp.concatenate(parts, axis=1)
    acc = jnp.dot(lhs, w_ref[...], preferred_element_type=F32)
    r = ALPHA * x_ref[...] + gate_ref[0] * acc
    o_ref[...] = _layernorm(r, lng_ref[...], lnb_ref[...])


def _outproj_odd(conv, sg, cn_g, cn_b, w_bf, x, mods, ln_g, ln_b, cond_of_tile, tm):
    rows, d = x.shape
    n_slab = E_C // LANES
    row_vec = pl.BlockSpec((1, d), lambda i: (0, 0))
    chan_vec = pl.BlockSpec((1, E_C), lambda i: (0, 0))
    return pl.pallas_call(
        functools.partial(_outproj_odd_kernel, n_slab=n_slab),
        grid=(rows // tm,),
        in_specs=[
            pl.BlockSpec((n_slab, tm, LANES), lambda i: (0, i, 0)),
            pl.BlockSpec((tm, E_C), lambda i: (i, 0)),
            chan_vec, chan_vec,
            _resident((E_C, d), lambda i: (0, 0)),
            pl.BlockSpec((tm, d), lambda i: (i, 0)),
            pl.BlockSpec((1, 1, d), lambda i: (cond_of_tile(i), 0, 2)),
            row_vec, row_vec,
        ],
        out_specs=pl.BlockSpec((tm, d), lambda i: (i, 0)),
        out_shape=jax.ShapeDtypeStruct((rows, d), F32),
        compiler_params=_params(("arbitrary",)),
        name="outproj_odd",
    )(conv, sg, cn_g.reshape(1, E_C), cn_b.reshape(1, E_C), w_bf, x, mods,
      ln_g.reshape(1, d), ln_b.reshape(1, d))


IN_ROWS = 1024
OUT_ROWS = 256
EVEN_COLS = 512
ODD_COLS = 256


def kernel(x_prompt, x_sample, cache_k, cache_v, c, c_ctx, w_ada, b_ada, ln_g, ln_b, w_in_even, conv_short_w,
           rpb, w_out_even, w_in_odd, conv_conf_w, conv_conf_b, ln_conf_g, ln_conf_b, w_out_odd):
    batch, seq, d = x_prompt.shape
    dec_batch, dec_seq, _ = x_sample.shape
    n_even, past = cache_k.shape[1], cache_k.shape[2]
    assert 1 + dec_batch <= N_COND and d == D_MODEL

    cond = jnp.zeros((N_COND, d), F32).at[0].set(c_ctx).at[1:1 + dec_batch].set(c)
    mods_all = _ada(cond, w_ada, b_ada)

    def cond_fn(is_prompt, tm):
        return (lambda i: 0) if is_prompt else (lambda i: 1 + (i * tm) // dec_seq)

    y_p = x_prompt.reshape(batch * seq, d)
    y_s = x_sample.reshape(dec_batch * dec_seq, d)
    streams = lambda: ((y_p, seq, True), (y_s, dec_seq, False))
    bf16_weights = {}

    def as_bf16(name, w):
        return bf16_weights[name] if name in bf16_weights else w.astype(BF16)

    new_k, new_v = [], []
    for layer in range(DEPTH):
        mods = mods_all[layer].reshape(N_COND, 1, 3 * d)
        half = layer // 2
        has_next = layer + 1 < DEPTH
        outs = []
        if layer % 2 == 0:
            side = {True: [("in_odd", w_in_odd)] if has_next else [],
                    False: [("out_even", w_out_even)] + ([("out_odd", w_out_odd)] if has_next else [])}
            w_in = w_in_even[half].astype(BF16)
            proj = {}
            for y, sq, is_prompt in streams():
                tm_in = min(y.shape[0], IN_ROWS)
                casts = [w[half] for _, w in side[is_prompt]]
                caches = () if is_prompt else (cache_k, cache_v)
                res = _inproj_even(y, mods, w_in, cond_fn(is_prompt, tm_in), tm_in, EVEN_COLS, casts, caches, half)
                proj[is_prompt] = res[:6]
                for (name, _), w16 in zip(side[is_prompt], res[6:6 + len(casts)]):
                    bf16_weights[(name, half)] = w16
                if not is_prompt:
                    k_ctx, v_ctx = (a.reshape(dec_batch, past, E_A) for a in res[6 + len(casts):])
            w_out = as_bf16(("out_even", half), w_out_even[half])
            for y, sq, is_prompt in streams():
                rest, k, v, q16, k16, v16 = proj[is_prompt]
                tm_out = min(y.shape[0], OUT_ROWS)
                if is_prompt:
                    oa = _attn_ctx(rest, q16, k16, v16, sq)
                    new_k.append(k.reshape(batch, seq, N_HEADS_A, HEAD_DIM))
                    new_v.append(v.reshape(batch, seq, N_HEADS_A, HEAD_DIM))
                else:
                    oa = _attn_nbr(rest, q16, k16, v16, k_ctx, v_ctx, rpb[half], sq, past, 0)
                outs.append(_outproj_even(oa, rest, conv_short_w[half], w_out, y, mods,
                                          ln_g[layer], ln_b[layer], cond_fn(is_prompt, tm_out), sq, tm_out))
        else:
            w_in = as_bf16(("in_odd", half), w_in_odd[half])
            w_out = as_bf16(("out_odd", half), w_out_odd[half])
            for y, sq, is_prompt in streams():
                tm_in = min(y.shape[0], IN_ROWS)
                tm_out = min(y.shape[0], OUT_ROWS)
                sg, conv = _inproj_odd(y, mods, w_in, conv_conf_w[half], conv_conf_b[half],
                                       cond_fn(is_prompt, tm_in), sq, tm_in, ODD_COLS)
                outs.append(_outproj_odd(conv, sg, ln_conf_g[half], ln_conf_b[half], w_out, y, mods,
                                         ln_g[layer], ln_b[layer], cond_fn(is_prompt, tm_out), tm_out))
        y_p, y_s = outs
    return (y_p.reshape(batch, seq, d), y_s.reshape(dec_batch, dec_seq, d),
            jnp.stack(new_k, axis=1), jnp.stack(new_v, axis=1))
```
